```python
import math
import jax, jax.numpy as jnp
from jax import lax
import numpy as np

D_MODEL = 1024
BATCH = 8
SEQ = 2048
DEPTH = 2

CHUNK = 64
N_A = DEPTH // 2
N_B = DEPTH - N_A
H_A = 8
DK_A = D_MODEL // H_A
DV_A = D_MODEL // H_A
QK_A = H_A * DK_A
V_A = H_A * DV_A
CONV_W = 4
A_IN = 2 * QK_A + 2 * V_A + 2 * H_A
H_B = 8
DH_B = D_MODEL // (2 * H_B)
QK_B = H_B * 2 * DH_B
V_B = H_B * 2 * DH_B
Q_BLOCK = 128
N_EXPERTS = 32
TOP_K = 4
D_EXPERT = D_MODEL
SWIGLU_LIMIT = 7.0
SWIGLU_ALPHA = 1.702
MOE_BLOCK = 256
EPS = 1e-6

kernel_name = "hybrid_deltanet_diffattn_moe_yoco"


def rms_norm(t):
    t32 = t.astype(jnp.float32)
    return t32 * lax.rsqrt(jnp.mean(t32 * t32, axis=-1, keepdims=True) + EPS)


def l2_normalize(t):
    return t * lax.rsqrt(jnp.sum(t * t, axis=-1, keepdims=True) + EPS)


def modulate(h, shift, scale):
    y = rms_norm(h) * (1.0 + scale[:, None, :].astype(jnp.float32)) + shift[:, None, :].astype(jnp.float32)
    return y.astype(h.dtype)


def causal_depthwise_conv(t, w):
    width = w.shape[0]
    return lax.conv_general_dilated(
        t, w[:, None, :].astype(t.dtype), window_strides=(1,), padding=[(width - 1, 0)],
        dimension_numbers=("NWC", "WIO", "NWC"), feature_group_count=t.shape[-1])


def chunk_gated_delta_rule(q, k, v, g, beta):
    bsz, seq, nh, dk = q.shape
    dv = v.shape[-1]
    nc = seq // CHUNK

    def chunks(t):
        t = jnp.moveaxis(t, 2, 1)
        return t.reshape((bsz, nh, nc, CHUNK) + t.shape[3:])

    q, k, v, g, beta = (chunks(t) for t in (q, k, v, g, beta))
    g = jnp.cumsum(g, axis=-1)
    idx = jnp.arange(CHUNK)
    incl = idx[:, None] >= idx[None, :]
    strict = idx[:, None] > idx[None, :]
    diff = g[..., :, None] - g[..., None, :]
    decay = jnp.where(incl, jnp.exp(jnp.where(incl, diff, 0.0)), 0.0)
    k_beta = k * beta[..., None]
    v_beta = v * beta[..., None]
    l_mat = jnp.where(strict, jnp.einsum('bhncd,bhnjd->bhncj', k_beta, k) * decay, 0.0)
    a_mat = l_mat + jnp.eye(CHUNK, dtype=l_mat.dtype)

    def solve(rhs):
        return lax.linalg.triangular_solve(a_mat, rhs, left_side=True, lower=True, unit_diagonal=True)

    u = solve(v_beta)
    w = solve(k_beta * jnp.exp(g)[..., None])
    qk = jnp.einsum('bhncd,bhnjd->bhncj', q, k) * decay
    q_dec = q * jnp.exp(g)[..., None]
    k_dec = k * jnp.exp(g[..., -1:] - g)[..., None]
    g_last = jnp.exp(g[..., -1])

    def step(state, xs):
        u_c, w_c, qk_c, q_c, k_c, gl_c = xs
        v_new = u_c - jnp.einsum('bhcd,bhde->bhce', w_c, state)
        o_c = jnp.einsum('bhcd,bhde->bhce', q_c, state) + jnp.einsum('bhcj,bhje->bhce', qk_c, v_new)
        state = state * gl_c[..., None, None] + jnp.einsum('bhcd,bhce->bhde', k_c, v_new)
        return state, o_c

    xs = tuple(jnp.moveaxis(t, 2, 0) for t in (u, w, qk, q_dec, k_dec, g_last))
    state0 = jnp.zeros((bsz, nh, dk, dv), jnp.float32)
    _, o = lax.scan(step, state0, xs)
    o = jnp.transpose(o, (1, 0, 3, 2, 4))
    return o.reshape(bsz, seq, nh, dv)


def gated_deltanet(xn, w_in, conv_w, a_log, dt_bias, norm_w, w_out):
    bsz, seq, _ = xn.shape
    proj = xn @ w_in
    n_qkv = 2 * QK_A + V_A
    qkv = jax.nn.silu(causal_depthwise_conv(proj[..., :n_qkv], conv_w)).astype(jnp.float32)
    z = proj[..., n_qkv:n_qkv + V_A].reshape(bsz, seq, H_A, DV_A)
    b_logit = proj[..., n_qkv + V_A:n_qkv + V_A + H_A].astype(jnp.float32)
    a_in = proj[..., n_qkv + V_A + H_A:].astype(jnp.float32)
    q = l2_normalize(qkv[..., :QK_A].reshape(bsz, seq, H_A, DK_A)) * (DK_A ** -0.5)
    k = l2_normalize(qkv[..., QK_A:2 * QK_A].reshape(bsz, seq, H_A, DK_A))
    v = qkv[..., 2 * QK_A:].reshape(bsz, seq, H_A, DV_A)
    beta = jax.nn.sigmoid(b_logit)
    g = -jnp.exp(a_log.astype(jnp.float32)) * jax.nn.softplus(a_in + dt_bias.astype(jnp.float32))
    o = chunk_gated_delta_rule(q, k, v, g, beta)
    o = rms_norm(o) * norm_w.astype(jnp.float32) * jax.nn.silu(z.astype(jnp.float32))
    return o.reshape(bsz, seq, V_A).astype(xn.dtype) @ w_out


def shared_kv(h, c_act, kv_ada_w, kv_ada_b, w_k, w_v, k_norm):
    bsz, seq, _ = h.shape
    shift, scale = jnp.split(c_act @ kv_ada_w + kv_ada_b, 2, axis=-1)
    hkv = modulate(h, shift, scale)
    k = rms_norm((hkv @ w_k).reshape(bsz, seq, H_B, 2, DH_B)) * k_norm.astype(jnp.float32)
    v = (hkv @ w_v).reshape(bsz, seq, H_B, 2 * DH_B).astype(jnp.float32)
    return k, v


def diff_attention(xn, k, v, w_q, q_norm, lam_q1, lam_k1, lam_q2, lam_k2, subln, w_out, lambda_init):
    bsz, seq, _ = xn.shape
    q = rms_norm((xn @ w_q).reshape(bsz, seq, H_B, 2, DH_B)) * q_norm.astype(jnp.float32) * (DH_B ** -0.5)
    lam = (jnp.exp(jnp.sum(lam_q1.astype(jnp.float32) * lam_k1.astype(jnp.float32)))
           - jnp.exp(jnp.sum(lam_q2.astype(jnp.float32) * lam_k2.astype(jnp.float32))) + lambda_init)
    nqb = seq // Q_BLOCK
    q_blocks = jnp.moveaxis(q.reshape(bsz, nqb, Q_BLOCK, H_B, 2, DH_B), 1, 0)
    k_chunk = jnp.arange(seq) // CHUNK

    def block(args):
        q_blk, bi = args
        s = jnp.einsum('bqhmd,bkhmd->bhmqk', q_blk, k)
        q_chunk = (bi * Q_BLOCK + jnp.arange(Q_BLOCK)) // CHUNK
        allowed = k_chunk[None, :] <= q_chunk[:, None]
        p = jax.nn.softmax(jnp.where(allowed, s, -1e30), axis=-1)
        a = p[:, :, 0] - lam * p[:, :, 1]
        return jnp.einsum('bhqk,bkhe->bqhe', a, v)

    o = lax.map(block, (q_blocks, jnp.arange(nqb)))
    o = jnp.moveaxis(o, 0, 1).reshape(bsz, seq, H_B, 2 * DH_B)
    o = rms_norm(o) * subln.astype(jnp.float32) * (1.0 - lambda_init)
    return o.reshape(bsz, seq, V_B).astype(xn.dtype) @ w_out


def clamped_swiglu(h_gate, h_up):
    g = jnp.minimum(h_gate, SWIGLU_LIMIT)
    u = jnp.clip(h_up, -SWIGLU_LIMIT, SWIGLU_LIMIT)
    return g * jax.nn.sigmoid(SWIGLU_ALPHA * g) * (u + 1.0)


def moe_ffn(xn, w_router, b_router, w_gate, b_gate, w_up, b_up, w_down, b_down):
    bsz, seq, d = xn.shape
    xt = xn.reshape(-1, d)
    n_tok = xt.shape[0]
    logits = (xt @ w_router + b_router).astype(jnp.float32)
    top_val, top_idx = lax.top_k(logits, TOP_K)
    gates = jax.nn.softmax(top_val, axis=-1)
    e_flat = top_idx.reshape(-1)
    g_flat = gates.reshape(-1)
    t_flat = jnp.arange(n_tok * TOP_K) // TOP_K
    order = jnp.argsort(e_flat)
    e_sorted = e_flat[order]
    counts = jnp.zeros((N_EXPERTS,), jnp.int32).at[e_flat].add(1)
    padded = (counts + MOE_BLOCK - 1) // MOE_BLOCK * MOE_BLOCK
    start = jnp.cumsum(counts) - counts
    pend = jnp.cumsum(padded)
    pstart = pend - padded
    dest = pstart[e_sorted] + (jnp.arange(n_tok * TOP_K) - start[e_sorted])
    n_blocks = (n_tok * TOP_K + N_EXPERTS * (MOE_BLOCK - 1) + MOE_BLOCK - 1) // MOE_BLOCK
    n_rows = n_blocks * MOE_BLOCK
    row_tok = jnp.zeros((n_rows,), jnp.int32).at[dest].set(t_flat[order])
    row_gate = jnp.zeros((n_rows,), jnp.float32).at[dest].set(g_flat[order])
    block_expert = jnp.minimum(
        jnp.searchsorted(pend, jnp.arange(n_blocks) * MOE_BLOCK, side='right'), N_EXPERTS - 1)

    def run_block(args):
        toks, e = args
        xb = xt[toks]
        hid = clamped_swiglu(xb @ w_gate[e] + b_gate[e], xb @ w_up[e] + b_up[e])
        return hid @ w_down[e] + b_down[e]

    ys = lax.map(run_block, (row_tok.reshape(n_blocks, MOE_BLOCK), block_expert))
    ys = (ys.reshape(n_rows, d) * row_gate[:, None]).astype(xt.dtype)
    out = jnp.zeros((n_tok, d), xt.dtype).at[row_tok].add(ys)
    return out.reshape(bsz, seq, d)


def _normal(k, shape, scale):
    return jax.random.normal(k, shape, jnp.float32) * scale


def setup_inputs(seed: int = 0) -> dict:
    key = jax.random.key(seed)
    ks = iter(jax.random.split(key, 40))
    D = D_MODEL
    x = _normal(next(ks), (BATCH, SEQ, D), 1.0)
    c = _normal(next(ks), (BATCH, D), 1.0)
    ada_w = _normal(next(ks), (DEPTH, D, 6 * D), D ** -0.5)
    ada_b = _normal(next(ks), (DEPTH, 6 * D), 0.02)
    a_w_in = _normal(next(ks), (N_A, D, A_IN), D ** -0.5)
    a_conv = _normal(next(ks), (N_A, CONV_W, 2 * QK_A + V_A), CONV_W ** -0.5)
    a_log = jnp.log(jax.random.uniform(next(ks), (N_A, H_A), jnp.float32, 1.0, 16.0))
    dt = jnp.exp(jax.random.uniform(next(ks), (N_A, H_A), jnp.float32, math.log(1e-3), math.log(1e-1)))
    a_dt_bias = dt + jnp.log(-jnp.expm1(-dt))
    a_norm = 1.0 + _normal(next(ks), (N_A, DV_A), 0.02)
    a_w_out = _normal(next(ks), (N_A, V_A, D), V_A ** -0.5)
    kv_ada_w = _normal(next(ks), (D, 2 * D), D ** -0.5)
    kv_ada_b = _normal(next(ks), (2 * D,), 0.02)
    b_w_k = _normal(next(ks), (D, QK_B), D ** -0.5)
    b_w_v = _normal(next(ks), (D, V_B), D ** -0.5)
    b_k_norm = 1.0 + _normal(next(ks), (DH_B,), 0.02)
    b_w_q = _normal(next(ks), (N_B, D, QK_B), D ** -0.5)
    b_q_norm = 1.0 + _normal(next(ks), (N_B, DH_B), 0.02)
    b_lam_q1 = _normal(next(ks), (N_B, DH_B), 0.1)
    b_lam_k1 = _normal(next(ks), (N_B, DH_B), 0.1)
    b_lam_q2 = _normal(next(ks), (N_B, DH_B), 0.1)
    b_lam_k2 = _normal(next(ks), (N_B, DH_B), 0.1)
    b_subln = 1.0 + _normal(next(ks), (N_B, 2 * DH_B), 0.02)
    b_w_out = _normal(next(ks), (N_B, V_B, D), V_B ** -0.5)
    moe_w_router = _normal(next(ks), (DEPTH, D, N_EXPERTS), D ** -0.5)
    moe_b_router = _normal(next(ks), (DEPTH, N_EXPERTS), 0.01)
    moe_w_gate = _normal(next(ks), (DEPTH, N_EXPERTS, D, D_EXPERT), D ** -0.5)
    moe_b_gate = _normal(next(ks), (DEPTH, N_EXPERTS, D_EXPERT), 0.02)
    moe_w_up = _normal(next(ks), (DEPTH, N_EXPERTS, D, D_EXPERT), D ** -0.5)
    moe_b_up = _normal(next(ks), (DEPTH, N_EXPERTS, D_EXPERT), 0.02)
    moe_w_down = _normal(next(ks), (DEPTH, N_EXPERTS, D_EXPERT, D), D_EXPERT ** -0.5)
    moe_b_down = _normal(next(ks), (DEPTH, N_EXPERTS, D), 0.02)
    return {"x": x, "c": c, "ada_w": ada_w, "ada_b": ada_b,
            "a_w_in": a_w_in, "a_conv": a_conv, "a_log": a_log, "a_dt_bias": a_dt_bias,
            "a_norm": a_norm, "a_w_out": a_w_out,
            "kv_ada_w": kv_ada_w, "kv_ada_b": kv_ada_b, "b_w_k": b_w_k, "b_w_v": b_w_v,
            "b_k_norm": b_k_norm, "b_w_q": b_w_q, "b_q_norm": b_q_norm,
            "b_lam_q1": b_lam_q1, "b_lam_k1": b_lam_k1, "b_lam_q2": b_lam_q2, "b_lam_k2": b_lam_k2,
            "b_subln": b_subln, "b_w_out": b_w_out,
            "moe_w_router": moe_w_router, "moe_b_router": moe_b_router,
            "moe_w_gate": moe_w_gate, "moe_b_gate": moe_b_gate,
            "moe_w_up": moe_w_up, "moe_b_up": moe_b_up,
            "moe_w_down": moe_w_down, "moe_b_down": moe_b_down}


def reference(x, c, ada_w, ada_b, a_w_in, a_conv, a_log, a_dt_bias, a_norm, a_w_out,
              kv_ada_w, kv_ada_b, b_w_k, b_w_v, b_k_norm, b_w_q, b_q_norm,
              b_lam_q1, b_lam_k1, b_lam_q2, b_lam_k2, b_subln, b_w_out,
              moe_w_router, moe_b_router, moe_w_gate, moe_b_gate, moe_w_up, moe_b_up,
              moe_w_down, moe_b_down):
    h = x
    c_act = jax.nn.silu(c)
    k_sh = None
    v_sh = None
    for l in range(DEPTH):
        mod = c_act @ ada_w[l] + ada_b[l]
        sh1, sc1, g1, sh2, sc2, g2 = jnp.split(mod, 6, axis=-1)
        xn = modulate(h, sh1, sc1)
        if l < N_A:
            y = gated_deltanet(xn, a_w_in[l], a_conv[l], a_log[l], a_dt_bias[l], a_norm[l], a_w_out[l])
        else:
            j = l - N_A
            lambda_init = 0.8 - 0.6 * math.exp(-0.3 * l)
            y = diff_attention(xn, k_sh, v_sh, b_w_q[j], b_q_norm[j], b_lam_q1[j], b_lam_k1[j],
                               b_lam_q2[j], b_lam_k2[j], b_subln[j], b_w_out[j], lambda_init)
        h = h + g1[:, None, :] * y
        xn = modulate(h, sh2, sc2)
        h = h + g2[:, None, :] * moe_ffn(xn, moe_w_router[l], moe_b_router[l], moe_w_gate[l],
                                           moe_b_gate[l], moe_w_up[l], moe_b_up[l],
                                           moe_w_down[l], moe_b_down[l])
        if l == N_A - 1:
            k_sh, v_sh = shared_kv(h, c_act, kv_ada_w, kv_ada_b, b_w_k, b_w_v, b_k_norm)
    return h
```

```python
import functools
import math

import jax
import jax.numpy as jnp
from jax import lax
from jax.experimental import pallas as pl
from jax.experimental.pallas import tpu as pltpu

F32 = jnp.float32
BF16 = jnp.bfloat16

D = 1024
CHUNK = 64
N_HEADS = 8
HEAD = 128
HALF = 64
CONV_W = 4
N_EXPERTS = 32
TOP_K = 4
SWIGLU_LIMIT = 7.0
SWIGLU_ALPHA = 1.702
EPS = 1e-6
NEG = -1e30

LANES = 128
SUBLANES = 8
SLABS = D // LANES
VMEM_LIMIT = 56 * 1024 * 1024

TM = 256
T_DELTA = 256
TQ = 256
BM = 256
TD = 128
IN_COLS = 4224


def _dot(a, b):
    return jnp.dot(a.astype(BF16), b.astype(BF16), preferred_element_type=F32)


def _dot_nt(a, b):
    return lax.dot_general(a.astype(BF16), b.astype(BF16), (((1,), (1,)), ((), ())),
                           preferred_element_type=F32)


def _split(a):
    hi = a.astype(BF16)
    lo = (a - hi.astype(F32)).astype(BF16)
    return hi, lo


def _dot3(a, b):
    a_hi, a_lo = _split(a)
    b_hi, b_lo = _split(b)
    d = functools.partial(jnp.dot, preferred_element_type=F32)
    return d(a_hi, b_hi) + (d(a_hi, b_lo) + d(a_lo, b_hi))


def _sigmoid(x):
    return 1.0 / (1.0 + jnp.exp(-x))


def _silu(x):
    return x * _sigmoid(x)


def _modulate(h, shift, scale):
    ms = jnp.mean(h * h, axis=-1, keepdims=True)
    return h * lax.rsqrt(ms + EPS) * (1.0 + scale) + shift


def _params(sem):
    return pltpu.CompilerParams(dimension_semantics=sem, vmem_limit_bytes=VMEM_LIMIT)


def _mod_kernel(c_ref, w_ref, b_ref, o_ref):
    o_ref[...] = _dot3(_silu(c_ref[...]), w_ref[...]) + b_ref[...]


def _mod_vectors(c, w, b, layer):
    n = w.shape[-1]
    tn = 1024
    if layer is None:
        w_spec = pl.BlockSpec((D, tn), lambda j: (0, j))
        b2 = b.reshape(1, n)
        b_spec = pl.BlockSpec((1, tn), lambda j: (0, j))
    else:
        w_spec = pl.BlockSpec((None, D, tn), lambda j: (layer, 0, j))
        b2 = b.reshape(b.shape[0], 1, n)
        b_spec = pl.BlockSpec((None, 1, tn), lambda j: (layer, 0, j))
    bsz = c.shape[0]
    return pl.pallas_call(
        _mod_kernel,
        out_shape=jax.ShapeDtypeStruct((bsz, n), F32),
        grid=(n // tn,),
        in_specs=[pl.BlockSpec((bsz, D), lambda j: (0, 0)), w_spec, b_spec],
        out_specs=pl.BlockSpec((bsz, tn), lambda j: (0, j)),
        compiler_params=_params(("arbitrary",)),
    )(c, w, b2)


def _inproj_kernel(h_ref, sh_ref, sc_ref, w_ref, o_ref):
    xn = _modulate(h_ref[...], sh_ref[0], sc_ref[0])
    o_ref[...] = jnp.dot(xn.astype(BF16), w_ref[...], preferred_element_type=F32)


def _inproj(h, shift, scale, w_bf16, seq):
    n_tok = h.shape[0]
    per_b = seq // TM
    vec = pl.BlockSpec((1, 1, D), lambda i: (i // per_b, 0, 0))
    return pl.pallas_call(
        _inproj_kernel,
        out_shape=jax.ShapeDtypeStruct((n_tok, IN_COLS), F32),
        grid=(n_tok // TM,),
        in_specs=[pl.BlockSpec((TM, D), lambda i: (i, 0)), vec, vec,
                  pl.BlockSpec((D, IN_COLS), lambda i: (0, 0))],
        out_specs=pl.BlockSpec((TM, IN_COLS), lambda i: (i, 0)),
        compiler_params=_params(("arbitrary",)),
    )(h, shift, scale, w_bf16)


def _conv_silu(x_ref, w_ref, pad_ref, t):
    pad_ref[pl.ds(SUBLANES, t), :] = x_ref[...]
    w = w_ref[...]
    y = w[3:4] * pad_ref[pl.ds(SUBLANES, t), :]
    for k in range(1, CONV_W):
        y = y + w[3 - k:4 - k] * pad_ref[pl.ds(SUBLANES - k, t), :]
    pad_ref[pl.ds(0, SUBLANES), :] = pad_ref[pl.ds(t, SUBLANES), :]
    return _silu(y)


def _delta_kernel(q_ref, k_ref, v_ref, z_ref, sm_ref, cq_ref, ck_ref, cv_ref, dec_ref, nw_ref,
                  tri_ref, o_ref, padq, padk, padv, state, *, t):
    hd = pl.program_id(1)

    @pl.when(pl.program_id(2) == 0)
    def _():
        zeros = jnp.zeros((SUBLANES, HEAD), F32)
        padq[pl.ds(0, SUBLANES), :] = zeros
        padk[pl.ds(0, SUBLANES), :] = zeros
        padv[pl.ds(0, SUBLANES), :] = zeros
        state[...] = jnp.zeros((HEAD, HEAD), F32)

    q = _conv_silu(q_ref, cq_ref, padq, t)
    k = _conv_silu(k_ref, ck_ref, padk, t)
    v = _conv_silu(v_ref, cv_ref, padv, t)
    q = q * lax.rsqrt(jnp.sum(q * q, axis=-1, keepdims=True) + EPS) * (HEAD ** -0.5)
    k = k * lax.rsqrt(jnp.sum(k * k, axis=-1, keepdims=True) + EPS)

    sm = sm_ref[...]
    lane = lax.broadcasted_iota(jnp.int32, (t, LANES), 1)
    beta = jnp.sum(jnp.where(lane == hd, _sigmoid(sm), 0.0), axis=-1, keepdims=True)
    dec = dec_ref[...]
    xs = sm + dec[1:2]
    softplus = jnp.maximum(xs, 0.0) + jnp.log(1.0 + jnp.exp(-jnp.abs(xs)))
    g_raw = -jnp.exp(dec[0:1]) * softplus
    g_cum = _dot3(tri_ref[...], g_raw)
    g_col = jnp.sum(jnp.where(lane == hd + N_HEADS, g_cum, 0.0), axis=-1, keepdims=True)
    g_t = g_cum.T
    sub = lax.broadcasted_iota(jnp.int32, (LANES, t), 0)
    g_row = jnp.sum(jnp.where(sub == hd + N_HEADS, g_t, 0.0), axis=0, keepdims=True)

    ri = lax.broadcasted_iota(jnp.int32, (CHUNK, CHUNK), 0)
    ci = lax.broadcasted_iota(jnp.int32, (CHUNK, CHUNK), 1)
    incl = ri >= ci
    strict = ri > ci
    eye = jnp.where(ri == ci, 1.0, 0.0)
    nw = nw_ref[...]

    for c in range(t // CHUNK):
        rows = slice(c * CHUNK, (c + 1) * CHUNK)
        qc, kc, vc = q[rows], k[rows], v[rows]
        bc, gc = beta[rows], g_col[rows]
        gr = g_row[:, c * CHUNK:(c + 1) * CHUNK]
        decay = jnp.where(incl, jnp.exp(jnp.where(incl, gc - gr, 0.0)), 0.0)
        eg = jnp.exp(gc)
        g_last = gc[CHUNK - 1:CHUNK]
        kb = kc * bc
        vb = vc * bc
        l_mat = jnp.where(strict, _dot_nt(kb, kc) * decay, 0.0)
        t_inv = eye - jnp.where((ri ^ ci) == 1, l_mat, 0.0)
        size = 2
        while size < CHUNK:
            shift = size.bit_length() - 1
            pair = ((ri >> (shift + 1)) == (ci >> (shift + 1))) & ((ri >> shift) != (ci >> shift))
            c_mat = jnp.where(pair, l_mat, 0.0)
            t_inv = t_inv - _dot3(_dot3(t_inv, c_mat), t_inv)
            size *= 2
        u = _dot(t_inv, vb)
        w = _dot(t_inv, kb * eg)
        qk = _dot_nt(qc, kc) * decay
        q_dec = qc * eg
        k_dec = kc * jnp.exp(g_last - gc)
        s = state[...]
        v_new = u - _dot(w, s)
        o = _dot(q_dec, s) + _dot(qk, v_new)
        state[...] = s * jnp.exp(g_last) + _dot(k_dec.T, v_new)
        zc = z_ref[pl.ds(c * CHUNK, CHUNK), :]
        o = o * lax.rsqrt(jnp.mean(o * o, axis=-1, keepdims=True) + EPS) * nw * _silu(zc)
        o_ref[pl.ds(c * CHUNK, CHUNK), :] = o.astype(o_ref.dtype)


def _chunk_tri(t):
    r = jnp.arange(t)
    same = (r[:, None] // CHUNK) == (r[None, :] // CHUNK)
    return jnp.where(same & (r[:, None] >= r[None, :]), 1.0, 0.0).astype(F32)


def _delta(proj, conv_w, dec, norm_w, bsz, seq):
    t = T_DELTA
    nt = seq // t
    n_tok = bsz * seq

    def col(off):
        return pl.BlockSpec((t, HEAD), lambda b, h, j: (b * nt + j, off + h))

    def cw(off):
        return pl.BlockSpec((CONV_W, HEAD), lambda b, h, j: (0, off + h))

    const = lambda shape: pl.BlockSpec(shape, lambda b, h, j: (0, 0))
    return pl.pallas_call(
        functools.partial(_delta_kernel, t=t),
        out_shape=jax.ShapeDtypeStruct((n_tok, D), BF16),
        grid=(bsz, N_HEADS, nt),
        in_specs=[col(0), col(N_HEADS), col(2 * N_HEADS), col(3 * N_HEADS),
                  pl.BlockSpec((t, LANES), lambda b, h, j: (b * nt + j, 4 * N_HEADS)),
                  cw(0), cw(N_HEADS), cw(2 * N_HEADS),
                  const((SUBLANES, LANES)), const((1, HEAD)), const((t, t))],
        out_specs=pl.BlockSpec((t, HEAD), lambda b, h, j: (b * nt + j, h)),
        scratch_shapes=[pltpu.VMEM((t + SUBLANES, HEAD), F32)] * 3 + [pltpu.VMEM((HEAD, HEAD), F32)],
        compiler_params=_params(("arbitrary", "arbitrary", "arbitrary")),
    )(proj, proj, proj, proj, proj, conv_w, conv_w, conv_w, dec, norm_w, _chunk_tri(t))


def _post_kernel(o_ref, w_ref, h_ref, g_ref, sh_ref, sc_ref, wr_ref, br_ref, tri_ref,
                 h1_ref, xs_ref, idx_ref, gate_ref, rank_ref, cnt_ref, carry, *, tm):
    @pl.when(pl.program_id(0) == 0)
    def _():
        carry[...] = jnp.zeros_like(carry)

    y = jnp.dot(o_ref[...], w_ref[...], preferred_element_type=F32)
    h1 = h_ref[...] + g_ref[0] * y
    h1_ref[...] = h1
    xn = _modulate(h1, sh_ref[0], sc_ref[0])
    for s in range(SLABS):
        xs_ref[pl.ds(s, tm, stride=SLABS), :] = xn[:, s * LANES:(s + 1) * LANES]

    lane = lax.broadcasted_iota(jnp.int32, (tm, LANES), 1)
    lane_f = lane.astype(F32)
    logits = jnp.where(lane < N_EXPERTS, _dot3(xn, wr_ref[...]) + br_ref[...], NEG)
    vals, onehots = [], []
    for _ in range(TOP_K):
        m = jnp.max(logits, axis=-1, keepdims=True)
        first = jnp.min(jnp.where(logits == m, lane_f, float(LANES)), axis=-1, keepdims=True)
        hit = lane_f == first
        vals.append(m)
        onehots.append(hit)
        logits = jnp.where(hit, 2.0 * NEG, logits)
    exps = [jnp.exp(v - vals[0]) for v in vals]
    denom = exps[0] + exps[1] + exps[2] + exps[3]

    member = jnp.zeros((tm, LANES), F32)
    for hit in onehots:
        member = jnp.where(hit, 1.0, member)
    before = jnp.dot(tri_ref[...], member.astype(BF16), preferred_element_type=F32) + carry[...]
    idx_out = jnp.zeros((tm, LANES), F32)
    gate_out = jnp.zeros((tm, LANES), F32)
    rank_out = jnp.zeros((tm, LANES), F32)
    for k in range(TOP_K):
        first = jnp.sum(jnp.where(onehots[k], lane_f, 0.0), axis=-1, keepdims=True)
        rank = jnp.sum(jnp.where(onehots[k], before, 0.0), axis=-1, keepdims=True)
        idx_out = jnp.where(lane == k, first, idx_out)
        gate_out = jnp.where(lane == k, exps[k] / denom, gate_out)
        rank_out = jnp.where(lane == k, rank, rank_out)
    idx_ref[...] = idx_out.astype(jnp.int32)
    gate_ref[...] = gate_out
    rank_ref[...] = rank_out.astype(jnp.int32)
    carry[...] = carry[...] + jnp.sum(member, axis=0, keepdims=True)
    cnt_ref[...] = jnp.broadcast_to(carry[...], cnt_ref.shape).astype(jnp.int32)


def _post(o, w_bf16, h, gate, shift, scale, w_router, b_router, seq):
    n_tok = h.shape[0]
    tm = TM
    per_b = seq // tm
    vec = pl.BlockSpec((1, 1, D), lambda i: (i // per_b, 0, 0))
    row = pl.BlockSpec((tm, D), lambda i: (i, 0))
    small = pl.BlockSpec((tm, LANES), lambda i: (i, 0))
    r = jnp.arange(tm)
    tri = jnp.where(r[:, None] > r[None, :], 1.0, 0.0).astype(BF16)
    wr = jnp.pad(w_router, ((0, 0), (0, LANES - N_EXPERTS)))
    br = jnp.pad(b_router, (0, LANES - N_EXPERTS)).reshape(1, LANES)
    return pl.pallas_call(
        functools.partial(_post_kernel, tm=tm),
        out_shape=(jax.ShapeDtypeStruct((n_tok, D), F32),
                   jax.ShapeDtypeStruct((n_tok * SLABS, LANES), F32),
                   jax.ShapeDtypeStruct((n_tok, LANES), jnp.int32),
                   jax.ShapeDtypeStruct((n_tok, LANES), F32),
                   jax.ShapeDtypeStruct((n_tok, LANES), jnp.int32),
                   jax.ShapeDtypeStruct((SUBLANES, LANES), jnp.int32)),
        grid=(n_tok // tm,),
        in_specs=[row, pl.BlockSpec((D, D), lambda i: (0, 0)), row, vec, vec, vec,
                  pl.BlockSpec((D, LANES), lambda i: (0, 0)),
                  pl.BlockSpec((1, LANES), lambda i: (0, 0)),
                  pl.BlockSpec((tm, tm), lambda i: (0, 0))],
        out_specs=(row, pl.BlockSpec((tm * SLABS, LANES), lambda i: (i, 0)), small, small, small,
                   pl.BlockSpec((SUBLANES, LANES), lambda i: (0, 0))),
        scratch_shapes=[pltpu.VMEM((1, LANES), F32)],
        compiler_params=_params(("arbitrary",)),
    )(o, w_bf16, h, gate, shift, scale, wr, br, tri)


def _row_copy(src, src_row, dst, dst_row, sem):
    return pltpu.make_async_copy(src.at[pl.ds(pl.multiple_of(src_row * SLABS, SLABS), SLABS)],
                                 dst.at[pl.ds(pl.multiple_of(dst_row * SLABS, SLABS), SLABS)], sem)


def _dispatch_kernel(pos_ref, pend_ref, cnt_ref, x_ref, out_ref, buf, zbuf, sems, zsem, *, td, n_tiles,
                     n_blocks):
    i = pl.program_id(0)
    slot = i % 2

    def wait_slot(s):
        for _ in range(TOP_K):
            pltpu.make_async_copy(buf.at[s], out_ref.at[pl.ds(0, td * SLABS)], sems.at[s]).wait()

    @pl.when(i == 0)
    def _():
        zbuf[...] = jnp.zeros_like(zbuf)

        def zero_copy(e):
            last = pl.multiple_of((pend_ref[e] - BM) * SLABS, BM * SLABS)
            return pltpu.make_async_copy(zbuf, out_ref.at[pl.ds(last, BM * SLABS)], zsem)

        def tail_copy(j):
            start = pl.multiple_of(j * (BM * SLABS), BM * SLABS)
            return pltpu.make_async_copy(zbuf, out_ref.at[pl.ds(start, BM * SLABS)], zsem)

        def tail_start(j, c):
            tail_copy(j).start()
            return c

        def tail_wait(j, c):
            tail_copy(j).wait()
            return c

        n_valid = pend_ref[N_EXPERTS - 1] // BM
        for e in range(N_EXPERTS):
            @pl.when(cnt_ref[e] > 0)
            def _():
                zero_copy(e).start()
        lax.fori_loop(n_valid, n_blocks, tail_start, 0)
        for e in range(N_EXPERTS):
            @pl.when(cnt_ref[e] > 0)
            def _():
                zero_copy(e).wait()
        lax.fori_loop(n_valid, n_blocks, tail_wait, 0)

    @pl.when(i >= 2)
    def _():
        wait_slot(slot)

    buf[slot] = x_ref[...]

    def body(t, carry):
        for k in range(TOP_K):
            p = pos_ref[(i * td + t) * TOP_K + k]
            _row_copy(buf.at[slot], t, out_ref, p, sems.at[slot]).start()
        return carry

    lax.fori_loop(0, td, body, 0)

    @pl.when(i == n_tiles - 1)
    def _():
        wait_slot(slot)
        if n_tiles > 1:
            wait_slot(1 - slot)


def _dispatch(pos_flat, pend, counts, xs, n_rows):
    n_tok = xs.shape[0] // SLABS
    td = TD
    n_tiles = n_tok // td
    return pl.pallas_call(
        functools.partial(_dispatch_kernel, td=td, n_tiles=n_tiles, n_blocks=n_rows // BM),
        out_shape=jax.ShapeDtypeStruct((n_rows * SLABS, LANES), F32),
        grid_spec=pltpu.PrefetchScalarGridSpec(
            num_scalar_prefetch=3,
            grid=(n_tiles,),
            in_specs=[pl.BlockSpec((td * SLABS, LANES), lambda i, *_: (i, 0))],
            out_specs=pl.BlockSpec(memory_space=pl.ANY),
            scratch_shapes=[pltpu.VMEM((2, td * SLABS, LANES), F32),
                            pltpu.VMEM((BM * SLABS, LANES), F32),
                            pltpu.SemaphoreType.DMA((2,)),
                            pltpu.SemaphoreType.DMA(())]),
        compiler_params=_params(("arbitrary",)),
    )(pos_flat, pend, counts, xs)


def _expert_kernel(be_ref, nv_ref, x_ref, wg_ref, bg_ref, wu_ref, bu_ref, wd_ref, bd_ref, y_ref,
                   wg_s, wu_s, wd_s):
    j = pl.program_id(0)

    @pl.when(j >= nv_ref[0])
    def _():
        y_ref[...] = jnp.zeros_like(y_ref)

    @pl.when(j < nv_ref[0])
    def _():
        prev = be_ref[jnp.maximum(j - 1, 0)]

        @pl.when((j == 0) | (be_ref[j] != prev))
        def _():
            wg_s[...] = wg_ref[...].astype(BF16)
            wu_s[...] = wu_ref[...].astype(BF16)
            wd_s[...] = wd_ref[...].astype(BF16)

        x = jnp.concatenate([x_ref[pl.ds(s, BM, stride=SLABS), :] for s in range(SLABS)],
                            axis=1).astype(BF16)
        hg = jnp.dot(x, wg_s[...], preferred_element_type=F32) + bg_ref[...]
        hu = jnp.dot(x, wu_s[...], preferred_element_type=F32) + bu_ref[...]
        g = jnp.minimum(hg, SWIGLU_LIMIT)
        u = jnp.clip(hu, -SWIGLU_LIMIT, SWIGLU_LIMIT)
        act = g * _sigmoid(SWIGLU_ALPHA * g) * (u + 1.0)
        y = jnp.dot(act.astype(BF16), wd_s[...], preferred_element_type=F32) + bd_ref[...]
        for s in range(SLABS):
            y_ref[pl.ds(s, BM, stride=SLABS), :] = y[:, s * LANES:(s + 1) * LANES]


def _experts(block_expert, n_valid, x_sorted, w_gate, b_gate, w_up, b_up, w_down, b_down, layer):
    n_blocks = block_expert.shape[0]

    def blk(j, be, nv):
        return (jnp.minimum(j, nv[0] - 1), 0)

    w_spec = pl.BlockSpec((None, None, D, D), lambda j, be, nv: (layer, be[j], 0, 0))
    b_spec = pl.BlockSpec((None, None, 1, D), lambda j, be, nv: (layer, be[j], 0, 0))
    r4 = lambda b: b.reshape(b.shape[0], N_EXPERTS, 1, D)
    return pl.pallas_call(
        _expert_kernel,
        out_shape=jax.ShapeDtypeStruct(x_sorted.shape, F32),
        grid_spec=pltpu.PrefetchScalarGridSpec(
            num_scalar_prefetch=2,
            grid=(n_blocks,),
            in_specs=[pl.BlockSpec((BM * SLABS, LANES), blk),
                      w_spec, b_spec, w_spec, b_spec, w_spec, b_spec],
            out_specs=pl.BlockSpec((BM * SLABS, LANES), lambda j, be, nv: (j, 0)),
            scratch_shapes=[pltpu.VMEM((D, D), BF16)] * 3),
        compiler_params=_params(("arbitrary",)),
    )(block_expert, n_valid, x_sorted, w_gate, r4(b_gate), w_up, r4(b_up), w_down, r4(b_down))


def _combine_kernel(pos_ref, y_ref, h_ref, gate_ref, g2_ref, o_ref, buf, sems, *, td, n_tiles):
    i = pl.program_id(0)
    slot = i % 2

    def issue(tile, s):
        def body(t, carry):
            for k in range(TOP_K):
                p = pos_ref[(tile * td + t) * TOP_K + k]
                _row_copy(y_ref, p, buf.at[s, k], t, sems.at[s]).start()
            return carry

        lax.fori_loop(0, td, body, 0)

    @pl.when(i == 0)
    def _():
        issue(0, 0)

    @pl.when(i + 1 < n_tiles)
    def _():
        issue(i + 1, 1 - slot)

    for k in range(TOP_K):
        pltpu.make_async_copy(y_ref.at[pl.ds(0, td * SLABS)], buf.at[slot, k], sems.at[slot]).wait()

    gates = gate_ref[...]
    acc = jnp.zeros((td, D), F32)
    for k in range(TOP_K):
        yk = jnp.concatenate([buf[slot, k, pl.ds(s, td, stride=SLABS), :] for s in range(SLABS)],
                             axis=1)
        acc = acc + gates[:, k:k + 1] * yk
    o_ref[...] = h_ref[...] + g2_ref[0] * acc


def _combine(pos_flat, y_sorted, h1, gates, g2, seq):
    n_tok = h1.shape[0]
    td = TD
    n_tiles = n_tok // td
    per_b = seq // td
    return pl.pallas_call(
        functools.partial(_combine_kernel, td=td, n_tiles=n_tiles),
        out_shape=jax.ShapeDtypeStruct((n_tok, D), F32),
        grid_spec=pltpu.PrefetchScalarGridSpec(
            num_scalar_prefetch=1,
            grid=(n_tiles,),
            in_specs=[pl.BlockSpec(memory_space=pl.ANY),
                      pl.BlockSpec((td, D), lambda i, p: (i, 0)),
                      pl.BlockSpec((td, LANES), lambda i, p: (i, 0)),
                      pl.BlockSpec((1, 1, D), lambda i, p: (i // per_b, 0, 0))],
            out_specs=pl.BlockSpec((td, D), lambda i, p: (i, 0)),
            scratch_shapes=[pltpu.VMEM((2, TOP_K, td * SLABS, LANES), F32),
                            pltpu.SemaphoreType.DMA((2,))]),
        compiler_params=_params(("arbitrary",)),
    )(pos_flat, y_sorted, h1, gates, g2)


def _moe(o, w_out_bf16, h, g1, sh2, sc2, g2, w_router, b_router, w_gate, b_gate, w_up, b_up,
         w_down, b_down, layer, seq):
    n_tok = h.shape[0]
    h1, xs, idx, gates, rank, cnt = _post(o, w_out_bf16, h, g1, sh2, sc2, w_router, b_router, seq)
    counts = cnt[0, :N_EXPERTS]
    padded = (counts + BM - 1) // BM * BM
    pend = jnp.cumsum(padded)
    pstart = pend - padded
    e_idx = idx[:, :TOP_K]
    pos_flat = (pstart[e_idx] + rank[:, :TOP_K]).reshape(-1)
    n_blocks = (n_tok * TOP_K + N_EXPERTS * (BM - 1)) // BM
    block_start = jnp.arange(n_blocks, dtype=jnp.int32) * BM
    block_expert = jnp.minimum(
        jnp.sum((pend[None, :] <= block_start[:, None]).astype(jnp.int32), axis=1), N_EXPERTS - 1)
    n_valid = (pend[-1:] // BM).astype(jnp.int32)
    x_sorted = _dispatch(pos_flat, pend, counts, xs, n_blocks * BM)
    y_sorted = _experts(block_expert, n_valid, x_sorted, w_gate, b_gate, w_up, b_up, w_down, b_down,
                        layer)
    return _combine(pos_flat, y_sorted, h1, gates, g2, seq)


def _group_rms(y, ind_ref, indt_ref):
    ss = jnp.dot((y * y).astype(BF16), ind_ref[...], preferred_element_type=F32)
    r = lax.rsqrt(ss * (1.0 / HALF) + EPS)
    r_hi, r_lo = _split(r)
    d = functools.partial(jnp.dot, preferred_element_type=F32)
    return y * (d(r_hi, indt_ref[...]) + d(r_lo, indt_ref[...]))


def _qkv_kernel(h_ref, ksh_ref, ksc_ref, qsh_ref, qsc_ref, wk_ref, wv_ref, wq_ref, kn_ref, qn_ref,
                ind_ref, indt_ref, q_ref, k_ref, v_ref):
    h = h_ref[...]
    hkv = _modulate(h, ksh_ref[0], ksc_ref[0]).astype(BF16)
    xn = _modulate(h, qsh_ref[0], qsc_ref[0]).astype(BF16)
    k = jnp.dot(hkv, wk_ref[...], preferred_element_type=F32)
    v = jnp.dot(hkv, wv_ref[...], preferred_element_type=F32)
    q = jnp.dot(xn, wq_ref[...], preferred_element_type=F32)
    k_ref[...] = (_group_rms(k, ind_ref, indt_ref) * kn_ref[...]).astype(BF16)
    q_ref[...] = (_group_rms(q, ind_ref, indt_ref) * qn_ref[...] * (HALF ** -0.5)).astype(BF16)
    v_ref[...] = v.astype(BF16)


def _qkv(h, kv_shift, kv_scale, q_shift, q_scale, w_k, w_v, w_q, k_norm, q_norm, seq):
    n_tok = h.shape[0]
    per_b = seq // TM
    vec = pl.BlockSpec((1, 1, D), lambda i: (i // per_b, 0, 0))
    row = pl.BlockSpec((TM, D), lambda i: (i, 0))
    full = lambda shape: pl.BlockSpec(shape, lambda i: (0, 0))
    groups = jnp.arange(D) // HALF
    ind = (groups[:, None] == jnp.arange(LANES)[None, :]).astype(BF16)
    out = jax.ShapeDtypeStruct((n_tok, D), BF16)
    return pl.pallas_call(
        _qkv_kernel,
        out_shape=(out, out, out),
        grid=(n_tok // TM,),
        in_specs=[row, vec, vec, vec, vec, full((D, D)), full((D, D)), full((D, D)),
                  full((1, D)), full((1, D)), full((D, LANES)), full((LANES, D))],
        out_specs=(row, row, row),
        compiler_params=_params(("arbitrary",)),
    )(h, kv_shift, kv_scale, q_shift, q_scale, w_k.astype(BF16), w_v.astype(BF16), w_q.astype(BF16),
      jnp.tile(k_norm, D // HALF).reshape(1, D), jnp.tile(q_norm, D // HALF).reshape(1, D),
      ind, ind.T)


def _attn_kernel(q_ref, k_ref, v_ref, lam_ref, sub_ref, o_ref, *, tq, lambda_init):
    qi = pl.program_id(2)
    q = q_ref[...]
    lane = lax.broadcasted_iota(jnp.int32, (tq, HEAD), 1)
    zero = jnp.zeros_like(q)
    qs = (jnp.where(lane < HALF, q, zero), jnp.where(lane >= HALF, q, zero))

    def tile(kj, carry, masked):
        start = pl.multiple_of(kj * tq, tq)
        k = k_ref[pl.ds(start, tq), :]
        v = v_ref[pl.ds(start, tq), :]
        out = []
        for m in range(2):
            m_prev, l_prev, acc_prev = carry[m]
            s = lax.dot_general(qs[m], k, (((1,), (1,)), ((), ())), preferred_element_type=F32)
            if masked:
                shift = CHUNK.bit_length() - 1
                r = lax.broadcasted_iota(jnp.int32, (tq, tq), 0) >> shift
                c = lax.broadcasted_iota(jnp.int32, (tq, tq), 1) >> shift
                s = jnp.where(c <= r, s, NEG)
            m_new = jnp.maximum(m_prev, jnp.max(s, axis=-1, keepdims=True))
            alpha = jnp.exp(m_prev - m_new)
            p = jnp.exp(s - m_new)
            l_new = alpha * l_prev + jnp.sum(p, axis=-1, keepdims=True)
            acc_new = alpha * acc_prev + jnp.dot(p.astype(BF16), v, preferred_element_type=F32)
            out.append((m_new, l_new, acc_new))
        return tuple(out)

    init = tuple((jnp.full((tq, 1), NEG, F32), jnp.zeros((tq, 1), F32), jnp.zeros((tq, HEAD), F32))
                 for _ in range(2))
    carry = lax.fori_loop(0, qi, lambda kj, c: tile(kj, c, False), init)
    (_, l0, a0), (_, l1, a1) = tile(qi, carry, True)

    lp = lam_ref[...]
    lam = (jnp.exp(jnp.sum(lp[0:1] * lp[1:2], axis=-1, keepdims=True))
           - jnp.exp(jnp.sum(lp[2:3] * lp[3:4], axis=-1, keepdims=True)) + lambda_init)
    o = a0 / l0 - lam * (a1 / l1)
    o = o * lax.rsqrt(jnp.mean(o * o, axis=-1, keepdims=True) + EPS) * sub_ref[...] * (1.0 - lambda_init)
    o_ref[...] = o.astype(o_ref.dtype)


def _attention(q, k, v, lam_params, subln, bsz, seq, lambda_init):
    tq = TQ
    nq = seq // tq
    n_tok = bsz * seq
    k3 = k.reshape(bsz, seq, D)
    v3 = v.reshape(bsz, seq, D)
    kv_spec = pl.BlockSpec((None, seq, HEAD), lambda b, h, i: (b, 0, h))
    return pl.pallas_call(
        functools.partial(_attn_kernel, tq=tq, lambda_init=lambda_init),
        out_shape=jax.ShapeDtypeStruct((n_tok, D), BF16),
        grid=(bsz, N_HEADS, nq),
        in_specs=[pl.BlockSpec((tq, HEAD), lambda b, h, i: (b * nq + i, h)), kv_spec, kv_spec,
                  pl.BlockSpec((SUBLANES, LANES), lambda b, h, i: (0, 0)),
                  pl.BlockSpec((1, HEAD), lambda b, h, i: (0, 0))],
        out_specs=pl.BlockSpec((tq, HEAD), lambda b, h, i: (b * nq + i, h)),
        compiler_params=_params(("arbitrary", "arbitrary", "arbitrary")),
    )(q, k3, v3, lam_params, subln.reshape(1, HEAD))


def kernel(x, c, ada_w, ada_b, a_w_in, a_conv, a_log, a_dt_bias, a_norm, a_w_out, kv_ada_w, kv_ada_b,
           b_w_k, b_w_v, b_k_norm, b_w_q, b_q_norm, b_lam_q1, b_lam_k1, b_lam_q2, b_lam_k2, b_subln,
           b_w_out, moe_w_router, moe_b_router, moe_w_gate, moe_b_gate, moe_w_up, moe_b_up,
           moe_w_down, moe_b_down):
    bsz, seq, _ = x.shape
    h = x.reshape(bsz * seq, D)

    def six(mod):
        return [mod[:, i * D:(i + 1) * D].reshape(bsz, 1, D) for i in range(6)]

    moe_args = (moe_w_router, moe_b_router, moe_w_gate, moe_b_gate, moe_w_up, moe_b_up,
                moe_w_down, moe_b_down)

    def moe(o, w_out, h, g1, sh2, sc2, g2, layer):
        return _moe(o, w_out.astype(BF16), h, g1, sh2, sc2, g2, moe_w_router[layer],
                    moe_b_router[layer], *moe_args[2:], layer, seq)

    sh1, sc1, g1, sh2, sc2, g2 = six(_mod_vectors(c, ada_w, ada_b, 0))
    n_in = a_w_in.shape[-1]
    w_in = jnp.pad(a_w_in[0], ((0, 0), (0, IN_COLS - n_in))).astype(BF16)
    proj = _inproj(h, sh1, sc1, w_in, seq)
    dec = jnp.zeros((SUBLANES, LANES), F32)
    dec = dec.at[0, N_HEADS:2 * N_HEADS].set(a_log[0]).at[1, N_HEADS:2 * N_HEADS].set(a_dt_bias[0])
    o = _delta(proj, a_conv[0], dec, a_norm[0].reshape(1, HEAD), bsz, seq)
    h = moe(o, a_w_out[0], h, g1, sh2, sc2, g2, 0)

    kv_mod = _mod_vectors(c, kv_ada_w, kv_ada_b, None)
    kv_shift = kv_mod[:, :D].reshape(bsz, 1, D)
    kv_scale = kv_mod[:, D:].reshape(bsz, 1, D)
    sh1, sc1, g1, sh2, sc2, g2 = six(_mod_vectors(c, ada_w, ada_b, 1))
    q, k, v = _qkv(h, kv_shift, kv_scale, sh1, sc1, b_w_k, b_w_v, b_w_q[0], b_k_norm, b_q_norm[0], seq)

    lambda_init = 0.8 - 0.6 * math.exp(-0.3 * 1)
    lam_params = jnp.zeros((SUBLANES, LANES), F32)
    for r, p in enumerate((b_lam_q1, b_lam_k1, b_lam_q2, b_lam_k2)):
        lam_params = lam_params.at[r, :HALF].set(p[0])
    o = _attention(q, k, v, lam_params, b_subln[0], bsz, seq, lambda_init)
    h = moe(o, b_w_out[0], h, g1, sh2, sc2, g2, 1)
    return h.reshape(bsz, seq, D)
```

```python
import functools
import math

import jax
import jax.numpy as jnp
from jax import lax
from jax.experimental import pallas as pl
from jax.experimental.pallas import tpu as pltpu

F32 = jnp.float32
BF16 = jnp.bfloat16

D = 1024
CHUNK = 64
N_HEADS = 8
HEAD = 128
HALF = 64
CONV_W = 4
N_EXPERTS = 32
TOP_K = 4
SWIGLU_LIMIT = 7.0
SWIGLU_ALPHA = 1.702
EPS = 1e-6
NEG = -1e30

LANES = 128
SUBLANES = 8
SLABS = D // LANES
VMEM_LIMIT = 56 * 1024 * 1024

TM = 256
T_DELTA = 256
TQ = 512
BM = 256
TD = 128
IN_COLS = 4224


def _dot(a, b):
    return jnp.dot(a.astype(BF16), b.astype(BF16), preferred_element_type=F32)


def _dot_nt(a, b):
    return lax.dot_general(a.astype(BF16), b.astype(BF16), (((1,), (1,)), ((), ())),
                           preferred_element_type=F32)


def _split(a):
    hi = a.astype(BF16)
    lo = (a - hi.astype(F32)).astype(BF16)
    return hi, lo


def _dot3(a, b):
    a_hi, a_lo = _split(a)
    b_hi, b_lo = _split(b)
    d = functools.partial(jnp.dot, preferred_element_type=F32)
    return d(a_hi, b_hi) + (d(a_hi, b_lo) + d(a_lo, b_hi))


def _sigmoid(x):
    return 1.0 / (1.0 + jnp.exp(-x))


def _silu(x):
    return x * _sigmoid(x)


def _modulate(h, shift, scale):
    ms = jnp.mean(h * h, axis=-1, keepdims=True)
    return h * lax.rsqrt(ms + EPS) * (1.0 + scale) + shift


def _params(sem):
    return pltpu.CompilerParams(dimension_semantics=sem, vmem_limit_bytes=VMEM_LIMIT)


def _mod_kernel(c_ref, w_ref, b_ref, o_ref):
    o_ref[...] = _dot3(_silu(c_ref[...]), w_ref[...]) + b_ref[...]


def _mod_vectors(c, w, b, layer):
    n = w.shape[-1]
    tn = 1024
    if layer is None:
        w_spec = pl.BlockSpec((D, tn), lambda j: (0, j))
        b2 = b.reshape(1, n)
        b_spec = pl.BlockSpec((1, tn), lambda j: (0, j))
    else:
        w_spec = pl.BlockSpec((None, D, tn), lambda j: (layer, 0, j))
        b2 = b.reshape(b.shape[0], 1, n)
        b_spec = pl.BlockSpec((None, 1, tn), lambda j: (layer, 0, j))
    bsz = c.shape[0]
    return pl.pallas_call(
        _mod_kernel,
        out_shape=jax.ShapeDtypeStruct((bsz, n), F32),
        grid=(n // tn,),
        in_specs=[pl.BlockSpec((bsz, D), lambda j: (0, 0)), w_spec, b_spec],
        out_specs=pl.BlockSpec((bsz, tn), lambda j: (0, j)),
        compiler_params=_params(("arbitrary",)),
    )(c, w, b2)


def _inproj_kernel(h_ref, sh_ref, sc_ref, w_ref, o_ref):
    xn = _modulate(h_ref[...], sh_ref[0], sc_ref[0])
    o_ref[...] = jnp.dot(xn.astype(BF16), w_ref[...], preferred_element_type=F32)


def _inproj(h, shift, scale, w_bf16, seq):
    n_tok = h.shape[0]
    per_b = seq // TM
    vec = pl.BlockSpec((1, 1, D), lambda i: (i // per_b, 0, 0))
    return pl.pallas_call(
        _inproj_kernel,
        out_shape=jax.ShapeDtypeStruct((n_tok, IN_COLS), F32),
        grid=(n_tok // TM,),
        in_specs=[pl.BlockSpec((TM, D), lambda i: (i, 0)), vec, vec,
                  pl.BlockSpec((D, IN_COLS), lambda i: (0, 0))],
        out_specs=pl.BlockSpec((TM, IN_COLS), lambda i: (i, 0)),
        compiler_params=_params(("arbitrary",)),
    )(h, shift, scale, w_bf16)


N_QKV = 3 * D


def _delta_kernel(qkv_ref, z_ref, sm_ref, cw_ref, dec_ref, nw_ref, tri_ref, o_ref, pad, state, *, t):
    @pl.when(pl.program_id(1) == 0)
    def _():
        pad[pl.ds(0, SUBLANES), :] = jnp.zeros((SUBLANES, N_QKV), F32)
        state[...] = jnp.zeros_like(state)

    pad[pl.ds(SUBLANES, t), :] = qkv_ref[...]

    sm = sm_ref[...]
    sig = _sigmoid(sm)
    dec = dec_ref[...]
    xs = sm + dec[1:2]
    softplus = jnp.maximum(xs, 0.0) + jnp.log(1.0 + jnp.exp(-jnp.abs(xs)))
    g_raw = -jnp.exp(dec[0:1]) * softplus
    g_cum = _dot3(tri_ref[...], g_raw)
    g_t = g_cum.T
    lane = lax.broadcasted_iota(jnp.int32, (t, LANES), 1)

    ri = lax.broadcasted_iota(jnp.int32, (CHUNK, CHUNK), 0)
    ci = lax.broadcasted_iota(jnp.int32, (CHUNK, CHUNK), 1)
    incl = ri >= ci
    strict = ri > ci
    eye = jnp.where(ri == ci, 1.0, 0.0)
    level_masks = []
    size = 2
    while size < CHUNK:
        shift = size.bit_length() - 1
        level_masks.append(((ri >> (shift + 1)) == (ci >> (shift + 1))) & ((ri >> shift) != (ci >> shift)))
        size *= 2
    nw = nw_ref[...]

    def conv_silu(col):
        cols = slice(col * HEAD, (col + 1) * HEAD)
        w = cw_ref[:, cols]
        y = w[3:4] * pad[pl.ds(SUBLANES, t), cols]
        for k in range(1, CONV_W):
            y = y + w[3 - k:4 - k] * pad[pl.ds(SUBLANES - k, t), cols]
        return _silu(y)

    n_chunks = t // CHUNK
    units = [(hd, c) for hd in range(N_HEADS) for c in range(n_chunks)]
    qs, ks, kbs, vbs, gcs, egs, grs = {}, {}, {}, {}, {}, {}, {}
    for hd in range(N_HEADS):
        q = conv_silu(hd)
        k = conv_silu(N_HEADS + hd)
        v = conv_silu(2 * N_HEADS + hd)
        q = q * lax.rsqrt(jnp.sum(q * q, axis=-1, keepdims=True) + EPS) * (HEAD ** -0.5)
        k = k * lax.rsqrt(jnp.sum(k * k, axis=-1, keepdims=True) + EPS)
        beta = jnp.sum(jnp.where(lane == hd, sig, 0.0), axis=-1, keepdims=True)
        g_col = jnp.sum(jnp.where(lane == hd + N_HEADS, g_cum, 0.0), axis=-1, keepdims=True)
        g_row = g_t[N_HEADS + hd:N_HEADS + hd + 1, :]
        kb = k * beta
        vb = v * beta
        eg = jnp.exp(g_col)
        for c in range(n_chunks):
            rows = slice(c * CHUNK, (c + 1) * CHUNK)
            u = (hd, c)
            qs[u], ks[u], kbs[u], vbs[u], gcs[u], egs[u] = q[rows], k[rows], kb[rows], vb[rows], g_col[rows], eg[rows]
            grs[u] = g_row[:, c * CHUNK:(c + 1) * CHUNK]

    decay = {u: jnp.where(incl, jnp.exp(jnp.where(incl, gcs[u] - grs[u], 0.0)), 0.0) for u in units}
    both = {u: _dot_nt(jnp.concatenate([kbs[u], qs[u]], axis=0), ks[u]) for u in units}
    l_mat = {u: jnp.where(strict, both[u][:CHUNK] * decay[u], 0.0) for u in units}
    qk = {u: both[u][CHUNK:] * decay[u] for u in units}
    t_inv = {u: eye - jnp.where((ri ^ ci) == 1, l_mat[u], 0.0) for u in units}
    for pair in level_masks:
        xc = {u: _dot(t_inv[u], jnp.where(pair, l_mat[u], 0.0)) for u in units}
        t_inv = {u: t_inv[u] - _dot(xc[u], t_inv[u]) for u in units}
    uw = {u: _dot(t_inv[u], jnp.concatenate([vbs[u], kbs[u] * egs[u]], axis=1)) for u in units}
    g_last = {u: gcs[u][CHUNK - 1:CHUNK] for u in units}
    k_dec_t = {u: (ks[u] * jnp.exp(g_last[u] - gcs[u])).T for u in units}
    q_dec = {u: qs[u] * egs[u] for u in units}

    heads = range(N_HEADS)
    s = [state[hd] for hd in heads]
    for c in range(n_chunks):
        ws = [_dot(jnp.concatenate([uw[hd, c][:, HEAD:], q_dec[hd, c]], axis=0), s[hd]) for hd in heads]
        v_new = [uw[hd, c][:, :HEAD] - ws[hd][:CHUNK] for hd in heads]
        o = [ws[hd][CHUNK:] + _dot(qk[hd, c], v_new[hd]) for hd in heads]
        s = [s[hd] * jnp.exp(g_last[hd, c]) + _dot(k_dec_t[hd, c], v_new[hd]) for hd in heads]
        for hd in heads:
            zc = z_ref[pl.ds(c * CHUNK, CHUNK), hd * HEAD:(hd + 1) * HEAD]
            on = o[hd] * lax.rsqrt(jnp.mean(o[hd] * o[hd], axis=-1, keepdims=True) + EPS) * nw * _silu(zc)
            o_ref[pl.ds(c * CHUNK, CHUNK), hd * HEAD:(hd + 1) * HEAD] = on.astype(o_ref.dtype)
    for hd in heads:
        state[hd] = s[hd]

    pad[pl.ds(0, SUBLANES), :] = pad[pl.ds(t, SUBLANES), :]


def _chunk_tri(t):
    r = jnp.arange(t)
    same = (r[:, None] // CHUNK) == (r[None, :] // CHUNK)
    return jnp.where(same & (r[:, None] >= r[None, :]), 1.0, 0.0).astype(F32)


def _delta(proj, conv_w, dec, norm_w, bsz, seq):
    t = T_DELTA
    nt = seq // t
    n_tok = bsz * seq
    const = lambda shape: pl.BlockSpec(shape, lambda b, j: (0, 0))
    return pl.pallas_call(
        functools.partial(_delta_kernel, t=t),
        out_shape=jax.ShapeDtypeStruct((n_tok, D), BF16),
        grid=(bsz, nt),
        in_specs=[pl.BlockSpec((t, N_QKV), lambda b, j: (b * nt + j, 0)),
                  pl.BlockSpec((t, D), lambda b, j: (b * nt + j, N_QKV // D)),
                  pl.BlockSpec((t, LANES), lambda b, j: (b * nt + j, (N_QKV + D) // LANES)),
                  const((CONV_W, N_QKV)), const((SUBLANES, LANES)), const((1, HEAD)), const((t, t))],
        out_specs=pl.BlockSpec((t, D), lambda b, j: (b * nt + j, 0)),
        scratch_shapes=[pltpu.VMEM((t + SUBLANES, N_QKV), F32),
                        pltpu.VMEM((N_HEADS, HEAD, HEAD), F32)],
        compiler_params=_params(("arbitrary", "arbitrary")),
    )(proj, proj, proj, conv_w, dec, norm_w, _chunk_tri(t))


def _post_kernel(o_ref, w_ref, h_ref, g_ref, sh_ref, sc_ref, wr_ref, br_ref, tri_ref,
                 h1_ref, xs_ref, idx_ref, gate_ref, rank_ref, cnt_ref, carry, *, tm):
    @pl.when(pl.program_id(0) == 0)
    def _():
        carry[...] = jnp.zeros_like(carry)

    y = jnp.dot(o_ref[...], w_ref[...], preferred_element_type=F32)
    h1 = h_ref[...] + g_ref[0] * y
    h1_ref[...] = h1
    xn = _modulate(h1, sh_ref[0], sc_ref[0])
    for s in range(SLABS):
        xs_ref[pl.ds(s, tm, stride=SLABS), :] = xn[:, s * LANES:(s + 1) * LANES]

    lane = lax.broadcasted_iota(jnp.int32, (tm, LANES), 1)
    lane_f = lane.astype(F32)
    logits = jnp.where(lane < N_EXPERTS, _dot3(xn, wr_ref[...]) + br_ref[...], NEG)
    vals, onehots = [], []
    for _ in range(TOP_K):
        m = jnp.max(logits, axis=-1, keepdims=True)
        first = jnp.min(jnp.where(logits == m, lane_f, float(LANES)), axis=-1, keepdims=True)
        hit = lane_f == first
        vals.append(m)
        onehots.append(hit)
        logits = jnp.where(hit, 2.0 * NEG, logits)
    exps = [jnp.exp(v - vals[0]) for v in vals]
    denom = exps[0] + exps[1] + exps[2] + exps[3]

    member = jnp.zeros((tm, LANES), F32)
    for hit in onehots:
        member = jnp.where(hit, 1.0, member)
    before = jnp.dot(tri_ref[...], member.astype(BF16), preferred_element_type=F32) + carry[...]
    idx_out = jnp.zeros((tm, LANES), F32)
    gate_out = jnp.zeros((tm, LANES), F32)
    rank_out = jnp.zeros((tm, LANES), F32)
    for k in range(TOP_K):
        first = jnp.sum(jnp.where(onehots[k], lane_f, 0.0), axis=-1, keepdims=True)
        rank = jnp.sum(jnp.where(onehots[k], before, 0.0), axis=-1, keepdims=True)
        idx_out = jnp.where(lane == k, first, idx_out)
        gate_out = jnp.where(lane == k, exps[k] / denom, gate_out)
        rank_out = jnp.where(lane == k, rank, rank_out)
    idx_ref[...] = idx_out.astype(jnp.int32)
    gate_ref[...] = gate_out
    rank_ref[...] = rank_out.astype(jnp.int32)
    carry[...] = carry[...] + jnp.sum(member, axis=0, keepdims=True)
    cnt_ref[...] = jnp.broadcast_to(carry[...], cnt_ref.shape).astype(jnp.int32)


def _post(o, w_bf16, h, gate, shift, scale, w_router, b_router, seq):
    n_tok = h.shape[0]
    tm = TM
    per_b = seq // tm
    vec = pl.BlockSpec((1, 1, D), lambda i: (i // per_b, 0, 0))
    row = pl.BlockSpec((tm, D), lambda i: (i, 0))
    small = pl.BlockSpec((tm, LANES), lambda i: (i, 0))
    r = jnp.arange(tm)
    tri = jnp.where(r[:, None] > r[None, :], 1.0, 0.0).astype(BF16)
    wr = jnp.pad(w_router, ((0, 0), (0, LANES - N_EXPERTS)))
    br = jnp.pad(b_router, (0, LANES - N_EXPERTS)).reshape(1, LANES)
    return pl.pallas_call(
        functools.partial(_post_kernel, tm=tm),
        out_shape=(jax.ShapeDtypeStruct((n_tok, D), F32),
                   jax.ShapeDtypeStruct((n_tok * SLABS, LANES), F32),
                   jax.ShapeDtypeStruct((n_tok, LANES), jnp.int32),
                   jax.ShapeDtypeStruct((n_tok, LANES), F32),
                   jax.ShapeDtypeStruct((n_tok, LANES), jnp.int32),
                   jax.ShapeDtypeStruct((SUBLANES, LANES), jnp.int32)),
        grid=(n_tok // tm,),
        in_specs=[row, pl.BlockSpec((D, D), lambda i: (0, 0)), row, vec, vec, vec,
                  pl.BlockSpec((D, LANES), lambda i: (0, 0)),
                  pl.BlockSpec((1, LANES), lambda i: (0, 0)),
                  pl.BlockSpec((tm, tm), lambda i: (0, 0))],
        out_specs=(row, pl.BlockSpec((tm * SLABS, LANES), lambda i: (i, 0)), small, small, small,
                   pl.BlockSpec((SUBLANES, LANES), lambda i: (0, 0))),
        scratch_shapes=[pltpu.VMEM((1, LANES), F32)],
        compiler_params=_params(("arbitrary",)),
    )(o, w_bf16, h, gate, shift, scale, wr, br, tri)


def _row_copy(src, src_row, dst, dst_row, sem):
    return pltpu.make_async_copy(src.at[pl.ds(pl.multiple_of(src_row * SLABS, SLABS), SLABS)],
                                 dst.at[pl.ds(pl.multiple_of(dst_row * SLABS, SLABS), SLABS)], sem)


def _dispatch_kernel(pos_ref, pend_ref, cnt_ref, x_ref, out_ref, buf, zbuf, sems, zsem, *, td, n_tiles,
                     n_blocks):
    i = pl.program_id(0)
    slot = i % 2

    def wait_slot(s):
        for _ in range(TOP_K):
            pltpu.make_async_copy(buf.at[s], out_ref.at[pl.ds(0, td * SLABS)], sems.at[s]).wait()

    @pl.when(i == 0)
    def _():
        zbuf[...] = jnp.zeros_like(zbuf)

        def zero_copy(e):
            last = pl.multiple_of((pend_ref[e] - BM) * SLABS, BM * SLABS)
            return pltpu.make_async_copy(zbuf, out_ref.at[pl.ds(last, BM * SLABS)], zsem)

        def tail_copy(j):
            start = pl.multiple_of(j * (BM * SLABS), BM * SLABS)
            return pltpu.make_async_copy(zbuf, out_ref.at[pl.ds(start, BM * SLABS)], zsem)

        def tail_start(j, c):
            tail_copy(j).start()
            return c

        def tail_wait(j, c):
            tail_copy(j).wait()
            return c

        n_valid = pend_ref[N_EXPERTS - 1] // BM
        for e in range(N_EXPERTS):
            @pl.when(cnt_ref[e] > 0)
            def _():
                zero_copy(e).start()
        lax.fori_loop(n_valid, n_blocks, tail_start, 0)
        for e in range(N_EXPERTS):
            @pl.when(cnt_ref[e] > 0)
            def _():
                zero_copy(e).wait()
        lax.fori_loop(n_valid, n_blocks, tail_wait, 0)

    @pl.when(i >= 2)
    def _():
        wait_slot(slot)

    buf[slot] = x_ref[...]

    def body(t, carry):
        for k in range(TOP_K):
            p = pos_ref[(i * td + t) * TOP_K + k]
            _row_copy(buf.at[slot], t, out_ref, p, sems.at[slot]).start()
        return carry

    lax.fori_loop(0, td, body, 0)

    @pl.when(i == n_tiles - 1)
    def _():
        wait_slot(slot)
        if n_tiles > 1:
            wait_slot(1 - slot)


def _dispatch(pos_flat, pend, counts, xs, n_rows):
    n_tok = xs.shape[0] // SLABS
    td = TD
    n_tiles = n_tok // td
    return pl.pallas_call(
        functools.partial(_dispatch_kernel, td=td, n_tiles=n_tiles, n_blocks=n_rows // BM),
        out_shape=jax.ShapeDtypeStruct((n_rows * SLABS, LANES), F32),
        grid_spec=pltpu.PrefetchScalarGridSpec(
            num_scalar_prefetch=3,
            grid=(n_tiles,),
            in_specs=[pl.BlockSpec((td * SLABS, LANES), lambda i, *_: (i, 0))],
            out_specs=pl.BlockSpec(memory_space=pl.ANY),
            scratch_shapes=[pltpu.VMEM((2, td * SLABS, LANES), F32),
                            pltpu.VMEM((BM * SLABS, LANES), F32),
                            pltpu.SemaphoreType.DMA((2,)),
                            pltpu.SemaphoreType.DMA(())]),
        compiler_params=_params(("arbitrary",)),
    )(pos_flat, pend, counts, xs)


def _expert_kernel(be_ref, nv_ref, x_ref, wg_ref, bg_ref, wu_ref, bu_ref, wd_ref, bd_ref, y_ref,
                   wg_s, wu_s, wd_s):
    j = pl.program_id(0)

    @pl.when(j >= nv_ref[0])
    def _():
        y_ref[...] = jnp.zeros_like(y_ref)

    @pl.when(j < nv_ref[0])
    def _():
        prev = be_ref[jnp.maximum(j - 1, 0)]

        @pl.when((j == 0) | (be_ref[j] != prev))
        def _():
            wg_s[...] = wg_ref[...].astype(BF16)
            wu_s[...] = wu_ref[...].astype(BF16)
            wd_s[...] = wd_ref[...].astype(BF16)

        x = jnp.concatenate([x_ref[pl.ds(s, BM, stride=SLABS), :] for s in range(SLABS)],
                            axis=1).astype(BF16)
        hg = jnp.dot(x, wg_s[...], preferred_element_type=F32) + bg_ref[...]
        hu = jnp.dot(x, wu_s[...], preferred_element_type=F32) + bu_ref[...]
        g = jnp.minimum(hg, SWIGLU_LIMIT)
        u = jnp.clip(hu, -SWIGLU_LIMIT, SWIGLU_LIMIT)
        act = g * _sigmoid(SWIGLU_ALPHA * g) * (u + 1.0)
        y = jnp.dot(act.astype(BF16), wd_s[...], preferred_element_type=F32) + bd_ref[...]
        for s in range(SLABS):
            y_ref[pl.ds(s, BM, stride=SLABS), :] = y[:, s * LANES:(s + 1) * LANES]


def _experts(block_expert, n_valid, x_sorted, w_gate, b_gate, w_up, b_up, w_down, b_down, layer):
    n_blocks = block_expert.shape[0]

    def blk(j, be, nv):
        return (jnp.minimum(j, nv[0] - 1), 0)

    w_spec = pl.BlockSpec((None, None, D, D), lambda j, be, nv: (layer, be[j], 0, 0))
    b_spec = pl.BlockSpec((None, None, 1, D), lambda j, be, nv: (layer, be[j], 0, 0))
    r4 = lambda b: b.reshape(b.shape[0], N_EXPERTS, 1, D)
    return pl.pallas_call(
        _expert_kernel,
        out_shape=jax.ShapeDtypeStruct(x_sorted.shape, F32),
        grid_spec=pltpu.PrefetchScalarGridSpec(
            num_scalar_prefetch=2,
            grid=(n_blocks,),
            in_specs=[pl.BlockSpec((BM * SLABS, LANES), blk),
                      w_spec, b_spec, w_spec, b_spec, w_spec, b_spec],
            out_specs=pl.BlockSpec((BM * SLABS, LANES), lambda j, be, nv: (j, 0)),
            scratch_shapes=[pltpu.VMEM((D, D), BF16)] * 3),
        compiler_params=_params(("arbitrary",)),
    )(block_expert, n_valid, x_sorted, w_gate, r4(b_gate), w_up, r4(b_up), w_down, r4(b_down))


def _combine_kernel(pos_ref, y_ref, h_ref, gate_ref, g2_ref, o_ref, buf, sems, *, td, n_tiles):
    i = pl.program_id(0)
    slot = i % 2

    def issue(tile, s):
        def body(t, carry):
            for k in range(TOP_K):
                p = pos_ref[(tile * td + t) * TOP_K + k]
                _row_copy(y_ref, p, buf.at[s, k], t, sems.at[s]).start()
            return carry

        lax.fori_loop(0, td, body, 0)

    @pl.when(i == 0)
    def _():
        issue(0, 0)

    @pl.when(i + 1 < n_tiles)
    def _():
        issue(i + 1, 1 - slot)

    for k in range(TOP_K):
        pltpu.make_async_copy(y_ref.at[pl.ds(0, td * SLABS)], buf.at[slot, k], sems.at[slot]).wait()

    gates = gate_ref[...]
    acc = jnp.zeros((td, D), F32)
    for k in range(TOP_K):
        yk = jnp.concatenate([buf[slot, k, pl.ds(s, td, stride=SLABS), :] for s in range(SLABS)],
                             axis=1)
        acc = acc + gates[:, k:k + 1] * yk
    o_ref[...] = h_ref[...] + g2_ref[0] * acc


def _combine(pos_flat, y_sorted, h1, gates, g2, seq):
    n_tok = h1.shape[0]
    td = TD
    n_tiles = n_tok // td
    per_b = seq // td
    return pl.pallas_call(
        functools.partial(_combine_kernel, td=td, n_tiles=n_tiles),
        out_shape=jax.ShapeDtypeStruct((n_tok, D), F32),
        grid_spec=pltpu.PrefetchScalarGridSpec(
            num_scalar_prefetch=1,
            grid=(n_tiles,),
            in_specs=[pl.BlockSpec(memory_space=pl.ANY),
                      pl.BlockSpec((td, D), lambda i, p: (i, 0)),
                      pl.BlockSpec((td, LANES), lambda i, p: (i, 0)),
                      pl.BlockSpec((1, 1, D), lambda i, p: (i // per_b, 0, 0))],
            out_specs=pl.BlockSpec((td, D), lambda i, p: (i, 0)),
            scratch_shapes=[pltpu.VMEM((2, TOP_K, td * SLABS, LANES), F32),
                            pltpu.SemaphoreType.DMA((2,))]),
        compiler_params=_params(("arbitrary",)),
    )(pos_flat, y_sorted, h1, gates, g2)


def _moe(o, w_out_bf16, h, g1, sh2, sc2, g2, w_router, b_router, w_gate, b_gate, w_up, b_up,
         w_down, b_down, layer, seq):
    n_tok = h.shape[0]
    h1, xs, idx, gates, rank, cnt = _post(o, w_out_bf16, h, g1, sh2, sc2, w_router, b_router, seq)
    counts = cnt[0, :N_EXPERTS]
    padded = (counts + BM - 1) // BM * BM
    pend = jnp.cumsum(padded)
    pstart = pend - padded
    e_idx = idx[:, :TOP_K]
    pos_flat = (pstart[e_idx] + rank[:, :TOP_K]).reshape(-1)
    n_blocks = (n_tok * TOP_K + N_EXPERTS * (BM - 1)) // BM
    block_start = jnp.arange(n_blocks, dtype=jnp.int32) * BM
    block_expert = jnp.minimum(
        jnp.sum((pend[None, :] <= block_start[:, None]).astype(jnp.int32), axis=1), N_EXPERTS - 1)
    n_valid = (pend[-1:] // BM).astype(jnp.int32)
    x_sorted = _dispatch(pos_flat, pend, counts, xs, n_blocks * BM)
    y_sorted = _experts(block_expert, n_valid, x_sorted, w_gate, b_gate, w_up, b_up, w_down, b_down,
                        layer)
    return _combine(pos_flat, y_sorted, h1, gates, g2, seq)


def _group_rms(y, ind_ref, indt_ref):
    ss = jnp.dot((y * y).astype(BF16), ind_ref[...], preferred_element_type=F32)
    r = lax.rsqrt(ss * (1.0 / HALF) + EPS)
    r_hi, r_lo = _split(r)
    d = functools.partial(jnp.dot, preferred_element_type=F32)
    return y * (d(r_hi, indt_ref[...]) + d(r_lo, indt_ref[...]))


def _qkv_kernel(h_ref, ksh_ref, ksc_ref, qsh_ref, qsc_ref, wk_ref, wv_ref, wq_ref, kn_ref, qn_ref,
                ind_ref, indt_ref, q_ref, k_ref, v_ref):
    h = h_ref[...]
    hkv = _modulate(h, ksh_ref[0], ksc_ref[0]).astype(BF16)
    xn = _modulate(h, qsh_ref[0], qsc_ref[0]).astype(BF16)
    k = jnp.dot(hkv, wk_ref[...], preferred_element_type=F32)
    v = jnp.dot(hkv, wv_ref[...], preferred_element_type=F32)
    q = jnp.dot(xn, wq_ref[...], preferred_element_type=F32)
    k_ref[...] = (_group_rms(k, ind_ref, indt_ref) * kn_ref[...]).astype(BF16)
    q_ref[...] = (_group_rms(q, ind_ref, indt_ref) * qn_ref[...] * (HALF ** -0.5)).astype(BF16)
    v_ref[...] = v.astype(BF16)


def _qkv(h, kv_shift, kv_scale, q_shift, q_scale, w_k, w_v, w_q, k_norm, q_norm, seq):
    n_tok = h.shape[0]
    per_b = seq // TM
    vec = pl.BlockSpec((1, 1, D), lambda i: (i // per_b, 0, 0))
    row = pl.BlockSpec((TM, D), lambda i: (i, 0))
    full = lambda shape: pl.BlockSpec(shape, lambda i: (0, 0))
    groups = jnp.arange(D) // HALF
    ind = (groups[:, None] == jnp.arange(LANES)[None, :]).astype(BF16)
    out = jax.ShapeDtypeStruct((n_tok, D), BF16)
    return pl.pallas_call(
        _qkv_kernel,
        out_shape=(out, out, out),
        grid=(n_tok // TM,),
        in_specs=[row, vec, vec, vec, vec, full((D, D)), full((D, D)), full((D, D)),
                  full((1, D)), full((1, D)), full((D, LANES)), full((LANES, D))],
        out_specs=(row, row, row),
        compiler_params=_params(("arbitrary",)),
    )(h, kv_shift, kv_scale, q_shift, q_scale, w_k.astype(BF16), w_v.astype(BF16), w_q.astype(BF16),
      jnp.tile(k_norm, D // HALF).reshape(1, D), jnp.tile(q_norm, D // HALF).reshape(1, D),
      ind, ind.T)


def _attn_kernel(q_ref, k_ref, v_ref, lam_ref, sub_ref, o_ref, *, tq, lambda_init):
    qi = pl.program_id(2)
    half = tq // 2
    lane = lax.broadcasted_iota(jnp.int32, (half, HEAD), 1)
    ones = jnp.ones((tq, HEAD), BF16)
    shift = CHUNK.bit_length() - 1
    diag_ok = ((lax.broadcasted_iota(jnp.int32, (half, half), 1) >> shift)
               <= (lax.broadcasted_iota(jnp.int32, (half, half), 0) >> shift))

    units = [(slab, m) for slab in range(2) for m in range(2)]
    qs = {}
    for slab, m in units:
        q = q_ref[pl.ds(slab * half, half), :]
        qs[slab, m] = jnp.where((lane < HALF) if m == 0 else (lane >= HALF), q, jnp.zeros_like(q))

    def scores(u, k):
        return lax.dot_general(qs[u], k, (((1,), (1,)), ((), ())), preferred_element_type=F32)

    def update(s, v1, carry):
        m_new = [jnp.maximum(carry[i][0], jnp.max(s[i], axis=-1, keepdims=True)) for i in range(4)]
        alpha = [jnp.exp(carry[i][0] - m_new[i]) for i in range(4)]
        p = [jnp.exp(s[i] - m_new[i]).astype(BF16) for i in range(4)]
        acc = [alpha[i] * carry[i][1] + jnp.dot(p[i], v1[i], preferred_element_type=F32) for i in range(4)]
        return tuple((m_new[i], acc[i]) for i in range(4))

    def kv_tile(kj):
        start = pl.multiple_of(kj * tq, tq)
        k = k_ref[pl.ds(start, tq), :]
        v1 = jnp.concatenate([v_ref[pl.ds(start, tq), :], ones], axis=1)
        return k, v1

    def full_tile(kj, carry):
        k, v1 = kv_tile(kj)
        return update([scores(u, k) for u in units], [v1] * 4, carry)

    def diag_tile(carry):
        k, v1 = kv_tile(qi)
        s, vs = [], []
        for u in units:
            if u[0] == 0:
                s.append(jnp.where(diag_ok, scores(u, k[:half]), NEG))
                vs.append(v1[:half])
            else:
                full = scores(u, k)
                s.append(jnp.concatenate([full[:, :half], jnp.where(diag_ok, full[:, half:], NEG)], axis=1))
                vs.append(v1)
        return update(s, vs, carry)

    init = tuple((jnp.full((half, 1), NEG, F32), jnp.zeros((half, 2 * HEAD), F32)) for _ in units)
    carry = diag_tile(lax.fori_loop(0, qi, full_tile, init))

    lp = lam_ref[...]
    lam = (jnp.exp(jnp.sum(lp[0:1] * lp[1:2], axis=-1, keepdims=True))
           - jnp.exp(jnp.sum(lp[2:3] * lp[3:4], axis=-1, keepdims=True)) + lambda_init)
    for slab in range(2):
        acc0, acc1 = carry[2 * slab][1], carry[2 * slab + 1][1]
        o = acc0[:, :HEAD] / acc0[:, HEAD:HEAD + 1] - lam * (acc1[:, :HEAD] / acc1[:, HEAD:HEAD + 1])
        o = o * lax.rsqrt(jnp.mean(o * o, axis=-1, keepdims=True) + EPS) * sub_ref[...] * (1.0 - lambda_init)
        o_ref[pl.ds(slab * half, half), :] = o.astype(o_ref.dtype)


def _attention(q, k, v, lam_params, subln, bsz, seq, lambda_init):
    tq = TQ
    nq = seq // tq
    n_tok = bsz * seq
    k3 = k.reshape(bsz, seq, D)
    v3 = v.reshape(bsz, seq, D)
    kv_spec = pl.BlockSpec((None, seq, HEAD), lambda b, h, i: (b, 0, h))
    return pl.pallas_call(
        functools.partial(_attn_kernel, tq=tq, lambda_init=lambda_init),
        out_shape=jax.ShapeDtypeStruct((n_tok, D), BF16),
        grid=(bsz, N_HEADS, nq),
        in_specs=[pl.BlockSpec((tq, HEAD), lambda b, h, i: (b * nq + i, h)), kv_spec, kv_spec,
                  pl.BlockSpec((SUBLANES, LANES), lambda b, h, i: (0, 0)),
                  pl.BlockSpec((1, HEAD), lambda b, h, i: (0, 0))],
        out_specs=pl.BlockSpec((tq, HEAD), lambda b, h, i: (b * nq + i, h)),
        compiler_params=_params(("arbitrary", "arbitrary", "arbitrary")),
    )(q, k3, v3, lam_params, subln.reshape(1, HEAD))


def kernel(x, c, ada_w, ada_b, a_w_in, a_conv, a_log, a_dt_bias, a_norm, a_w_out, kv_ada_w, kv_ada_b,
           b_w_k, b_w_v, b_k_norm, b_w_q, b_q_norm, b_lam_q1, b_lam_k1, b_lam_q2, b_lam_k2, b_subln,
           b_w_out, moe_w_router, moe_b_router, moe_w_gate, moe_b_gate, moe_w_up, moe_b_up,
           moe_w_down, moe_b_down):
    bsz, seq, _ = x.shape
    h = x.reshape(bsz * seq, D)

    def six(mod):
        return [mod[:, i * D:(i + 1) * D].reshape(bsz, 1, D) for i in range(6)]

    moe_args = (moe_w_router, moe_b_router, moe_w_gate, moe_b_gate, moe_w_up, moe_b_up,
                moe_w_down, moe_b_down)

    def moe(o, w_out, h, g1, sh2, sc2, g2, layer):
        return _moe(o, w_out.astype(BF16), h, g1, sh2, sc2, g2, moe_w_router[layer],
                    moe_b_router[layer], *moe_args[2:], layer, seq)

    sh1, sc1, g1, sh2, sc2, g2 = six(_mod_vectors(c, ada_w, ada_b, 0))
    n_in = a_w_in.shape[-1]
    w_in = jnp.pad(a_w_in[0], ((0, 0), (0, IN_COLS - n_in))).astype(BF16)
    proj = _inproj(h, sh1, sc1, w_in, seq)
    dec = jnp.zeros((SUBLANES, LANES), F32)
    dec = dec.at[0, N_HEADS:2 * N_HEADS].set(a_log[0]).at[1, N_HEADS:2 * N_HEADS].set(a_dt_bias[0])
    o = _delta(proj, a_conv[0], dec, a_norm[0].reshape(1, HEAD), bsz, seq)
    h = moe(o, a_w_out[0], h, g1, sh2, sc2, g2, 0)

    kv_mod = _mod_vectors(c, kv_ada_w, kv_ada_b, None)
    kv_shift = kv_mod[:, :D].reshape(bsz, 1, D)
    kv_scale = kv_mod[:, D:].reshape(bsz, 1, D)
    sh1, sc1, g1, sh2, sc2, g2 = six(_mod_vectors(c, ada_w, ada_b, 1))
    q, k, v = _qkv(h, kv_shift, kv_scale, sh1, sc1, b_w_k, b_w_v, b_w_q[0], b_k_norm, b_q_norm[0], seq)

    lambda_init = 0.8 - 0.6 * math.exp(-0.3 * 1)
    lam_params = jnp.zeros((SUBLANES, LANES), F32)
    for r, p in enumerate((b_lam_q1, b_lam_k1, b_lam_q2, b_lam_k2)):
        lam_params = lam_params.at[r, :HALF].set(p[0])
    o = _attention(q, k, v, lam_params, b_subln[0], bsz, seq, lambda_init)
    h = moe(o, b_w_out[0], h, g1, sh2, sc2, g2, 1)
    return h.reshape(bsz, seq, D)
```

```python
import functools
import math

import jax
import jax.numpy as jnp
from jax import lax
from jax.experimental import pallas as pl
from jax.experimental.pallas import tpu as pltpu

F32 = jnp.float32
BF16 = jnp.bfloat16

D = 1024
CHUNK = 64
N_HEADS = 8
HEAD = 128
HALF = 64
CONV_W = 4
N_EXPERTS = 32
TOP_K = 4
SWIGLU_LIMIT = 7.0
SWIGLU_ALPHA = 1.702
EPS = 1e-6
NEG = -1e30

LANES = 128
SUBLANES = 8
SLABS = D // LANES
VMEM_LIMIT = 56 * 1024 * 1024

TM = 256
T_DELTA = 256
TQ = 512
BM = 512
TD = 128
IN_COLS = 4224


def _dot(a, b):
    return jnp.dot(a.astype(BF16), b.astype(BF16), preferred_element_type=F32)


def _dot_nt(a, b):
    return lax.dot_general(a.astype(BF16), b.astype(BF16), (((1,), (1,)), ((), ())),
                           preferred_element_type=F32)


def _split(a):
    hi = a.astype(BF16)
    lo = (a - hi.astype(F32)).astype(BF16)
    return hi, lo


def _dot3(a, b):
    a_hi, a_lo = _split(a)
    b_hi, b_lo = _split(b)
    d = functools.partial(jnp.dot, preferred_element_type=F32)
    return d(a_hi, b_hi) + (d(a_hi, b_lo) + d(a_lo, b_hi))


def _sigmoid(x):
    return 1.0 / (1.0 + jnp.exp(-x))


def _silu(x):
    return x * _sigmoid(x)


def _modulate(h, shift, scale):
    ms = jnp.mean(h * h, axis=-1, keepdims=True)
    return h * lax.rsqrt(ms + EPS) * (1.0 + scale) + shift


def _params(sem):
    return pltpu.CompilerParams(dimension_semantics=sem, vmem_limit_bytes=VMEM_LIMIT)


def _mod_kernel(c_ref, w_ref, b_ref, o_ref):
    o_ref[...] = _dot3(_silu(c_ref[...]), w_ref[...]) + b_ref[...]


def _mod_vectors(c, w, b, layer):
    n = w.shape[-1]
    tn = 1024
    if layer is None:
        w_spec = pl.BlockSpec((D, tn), lambda j: (0, j))
        b2 = b.reshape(1, n)
        b_spec = pl.BlockSpec((1, tn), lambda j: (0, j))
    else:
        w_spec = pl.BlockSpec((None, D, tn), lambda j: (layer, 0, j))
        b2 = b.reshape(b.shape[0], 1, n)
        b_spec = pl.BlockSpec((None, 1, tn), lambda j: (layer, 0, j))
    bsz = c.shape[0]
    return pl.pallas_call(
        _mod_kernel,
        out_shape=jax.ShapeDtypeStruct((bsz, n), F32),
        grid=(n // tn,),
        in_specs=[pl.BlockSpec((bsz, D), lambda j: (0, 0)), w_spec, b_spec],
        out_specs=pl.BlockSpec((bsz, tn), lambda j: (0, j)),
        compiler_params=_params(("arbitrary",)),
    )(c, w, b2)


def _inproj_kernel(h_ref, sh_ref, sc_ref, w_ref, o_ref):
    xn = _modulate(h_ref[...], sh_ref[0], sc_ref[0])
    o_ref[...] = jnp.dot(xn.astype(BF16), w_ref[...], preferred_element_type=F32)


def _inproj(h, shift, scale, w_bf16, seq):
    n_tok = h.shape[0]
    per_b = seq // TM
    vec = pl.BlockSpec((1, 1, D), lambda i: (i // per_b, 0, 0))
    return pl.pallas_call(
        _inproj_kernel,
        out_shape=jax.ShapeDtypeStruct((n_tok, IN_COLS), F32),
        grid=(n_tok // TM,),
        in_specs=[pl.BlockSpec((TM, D), lambda i: (i, 0)), vec, vec,
                  pl.BlockSpec((D, IN_COLS), lambda i: (0, 0))],
        out_specs=pl.BlockSpec((TM, IN_COLS), lambda i: (i, 0)),
        compiler_params=_params(("arbitrary",)),
    )(h, shift, scale, w_bf16)


N_QKV = 3 * D


def _delta_kernel(qkv_ref, z_ref, sm_ref, cw_ref, dec_ref, nw_ref, tri_ref, o_ref, pad, state, *, t):
    @pl.when(pl.program_id(1) == 0)
    def _():
        pad[pl.ds(0, SUBLANES), :] = jnp.zeros((SUBLANES, N_QKV), F32)
        state[...] = jnp.zeros_like(state)

    pad[pl.ds(SUBLANES, t), :] = qkv_ref[...]

    sm = sm_ref[...]
    sig = _sigmoid(sm)
    dec = dec_ref[...]
    xs = sm + dec[1:2]
    softplus = jnp.maximum(xs, 0.0) + jnp.log(1.0 + jnp.exp(-jnp.abs(xs)))
    g_raw = -jnp.exp(dec[0:1]) * softplus
    g_cum = _dot3(tri_ref[...], g_raw)
    g_t = g_cum.T
    lane = lax.broadcasted_iota(jnp.int32, (t, LANES), 1)

    ri = lax.broadcasted_iota(jnp.int32, (CHUNK, CHUNK), 0)
    ci = lax.broadcasted_iota(jnp.int32, (CHUNK, CHUNK), 1)
    incl = ri >= ci
    strict = ri > ci
    eye = jnp.where(ri == ci, 1.0, 0.0)
    level_masks = []
    size = 2
    while size < CHUNK:
        shift = size.bit_length() - 1
        level_masks.append(((ri >> (shift + 1)) == (ci >> (shift + 1))) & ((ri >> shift) != (ci >> shift)))
        size *= 2
    nw = nw_ref[...]

    def conv_silu(col):
        cols = slice(col * HEAD, (col + 1) * HEAD)
        w = cw_ref[:, cols]
        y = w[3:4] * pad[pl.ds(SUBLANES, t), cols]
        for k in range(1, CONV_W):
            y = y + w[3 - k:4 - k] * pad[pl.ds(SUBLANES - k, t), cols]
        return _silu(y)

    n_chunks = t // CHUNK
    units = [(hd, c) for hd in range(N_HEADS) for c in range(n_chunks)]
    qs, ks, kbs, vbs, gcs, egs, grs = {}, {}, {}, {}, {}, {}, {}
    for hd in range(N_HEADS):
        q = conv_silu(hd)
        k = conv_silu(N_HEADS + hd)
        v = conv_silu(2 * N_HEADS + hd)
        q = q * lax.rsqrt(jnp.sum(q * q, axis=-1, keepdims=True) + EPS) * (HEAD ** -0.5)
        k = k * lax.rsqrt(jnp.sum(k * k, axis=-1, keepdims=True) + EPS)
        beta = jnp.sum(jnp.where(lane == hd, sig, 0.0), axis=-1, keepdims=True)
        g_col = jnp.sum(jnp.where(lane == hd + N_HEADS, g_cum, 0.0), axis=-1, keepdims=True)
        g_row = g_t[N_HEADS + hd:N_HEADS + hd + 1, :]
        kb = k * beta
        vb = v * beta
        eg = jnp.exp(g_col)
        for c in range(n_chunks):
            rows = slice(c * CHUNK, (c + 1) * CHUNK)
            u = (hd, c)
            qs[u], ks[u], kbs[u], vbs[u], gcs[u], egs[u] = q[rows], k[rows], kb[rows], vb[rows], g_col[rows], eg[rows]
            grs[u] = g_row[:, c * CHUNK:(c + 1) * CHUNK]

    decay = {u: jnp.where(incl, jnp.exp(jnp.where(incl, gcs[u] - grs[u], 0.0)), 0.0) for u in units}
    both = {u: _dot_nt(jnp.concatenate([kbs[u], qs[u]], axis=0), ks[u]) for u in units}
    l_mat = {u: jnp.where(strict, both[u][:CHUNK] * decay[u], 0.0) for u in units}
    qk = {u: both[u][CHUNK:] * decay[u] for u in units}
    t_inv = {u: eye - jnp.where((ri ^ ci) == 1, l_mat[u], 0.0) for u in units}
    for pair in level_masks:
        xc = {u: _dot(t_inv[u], jnp.where(pair, l_mat[u], 0.0)) for u in units}
        t_inv = {u: t_inv[u] - _dot(xc[u], t_inv[u]) for u in units}
    uw = {u: _dot(t_inv[u], jnp.concatenate([vbs[u], kbs[u] * egs[u]], axis=1)) for u in units}
    g_last = {u: gcs[u][CHUNK - 1:CHUNK] for u in units}
    k_dec_t = {u: (ks[u] * jnp.exp(g_last[u] - gcs[u])).T for u in units}
    q_dec = {u: qs[u] * egs[u] for u in units}

    heads = range(N_HEADS)
    s = [state[hd] for hd in heads]
    for c in range(n_chunks):
        ws = [_dot(jnp.concatenate([uw[hd, c][:, HEAD:], q_dec[hd, c]], axis=0), s[hd]) for hd in heads]
        v_new = [uw[hd, c][:, :HEAD] - ws[hd][:CHUNK] for hd in heads]
        o = [ws[hd][CHUNK:] + _dot(qk[hd, c], v_new[hd]) for hd in heads]
        s = [s[hd] * jnp.exp(g_last[hd, c]) + _dot(k_dec_t[hd, c], v_new[hd]) for hd in heads]
        for hd in heads:
            zc = z_ref[pl.ds(c * CHUNK, CHUNK), hd * HEAD:(hd + 1) * HEAD]
            on = o[hd] * lax.rsqrt(jnp.mean(o[hd] * o[hd], axis=-1, keepdims=True) + EPS) * nw * _silu(zc)
            o_ref[pl.ds(c * CHUNK, CHUNK), hd * HEAD:(hd + 1) * HEAD] = on.astype(o_ref.dtype)
    for hd in heads:
        state[hd] = s[hd]

    pad[pl.ds(0, SUBLANES), :] = pad[pl.ds(t, SUBLANES), :]


def _chunk_tri(t):
    r = jnp.arange(t)
    same = (r[:, None] // CHUNK) == (r[None, :] // CHUNK)
    return jnp.where(same & (r[:, None] >= r[None, :]), 1.0, 0.0).astype(F32)


def _delta(proj, conv_w, dec, norm_w, bsz, seq):
    t = T_DELTA
    nt = seq // t
    n_tok = bsz * seq
    const = lambda shape: pl.BlockSpec(shape, lambda b, j: (0, 0))
    return pl.pallas_call(
        functools.partial(_delta_kernel, t=t),
        out_shape=jax.ShapeDtypeStruct((n_tok, D), BF16),
        grid=(bsz, nt),
        in_specs=[pl.BlockSpec((t, N_QKV), lambda b, j: (b * nt + j, 0)),
                  pl.BlockSpec((t, D), lambda b, j: (b * nt + j, N_QKV // D)),
                  pl.BlockSpec((t, LANES), lambda b, j: (b * nt + j, (N_QKV + D) // LANES)),
                  const((CONV_W, N_QKV)), const((SUBLANES, LANES)), const((1, HEAD)), const((t, t))],
        out_specs=pl.BlockSpec((t, D), lambda b, j: (b * nt + j, 0)),
        scratch_shapes=[pltpu.VMEM((t + SUBLANES, N_QKV), F32),
                        pltpu.VMEM((N_HEADS, HEAD, HEAD), F32)],
        compiler_params=_params(("arbitrary", "arbitrary")),
    )(proj, proj, proj, conv_w, dec, norm_w, _chunk_tri(t))


def _post_kernel(o_ref, w_ref, h_ref, g_ref, sh_ref, sc_ref, wr_ref, br_ref, tri_ref,
                 h1_ref, xs_ref, idx_ref, gate_ref, rank_ref, cnt_ref, carry, *, tm):
    @pl.when(pl.program_id(0) == 0)
    def _():
        carry[...] = jnp.zeros_like(carry)

    y = jnp.dot(o_ref[...], w_ref[...], preferred_element_type=F32)
    h1 = h_ref[...] + g_ref[0] * y
    h1_ref[...] = h1
    xn = _modulate(h1, sh_ref[0], sc_ref[0])
    for s in range(SLABS):
        xs_ref[pl.ds(s, tm, stride=SLABS), :] = xn[:, s * LANES:(s + 1) * LANES]

    lane = lax.broadcasted_iota(jnp.int32, (tm, LANES), 1)
    lane_f = lane.astype(F32)
    logits = jnp.where(lane < N_EXPERTS, _dot3(xn, wr_ref[...]) + br_ref[...], NEG)
    vals, onehots = [], []
    for _ in range(TOP_K):
        m = jnp.max(logits, axis=-1, keepdims=True)
        first = jnp.min(jnp.where(logits == m, lane_f, float(LANES)), axis=-1, keepdims=True)
        hit = lane_f == first
        vals.append(m)
        onehots.append(hit)
        logits = jnp.where(hit, 2.0 * NEG, logits)
    exps = [jnp.exp(v - vals[0]) for v in vals]
    denom = exps[0] + exps[1] + exps[2] + exps[3]

    member = jnp.zeros((tm, LANES), F32)
    for hit in onehots:
        member = jnp.where(hit, 1.0, member)
    before = jnp.dot(tri_ref[...], member.astype(BF16), preferred_element_type=F32) + carry[...]
    idx_out = jnp.zeros((tm, LANES), F32)
    gate_out = jnp.zeros((tm, LANES), F32)
    rank_out = jnp.zeros((tm, LANES), F32)
    for k in range(TOP_K):
        first = jnp.sum(jnp.where(onehots[k], lane_f, 0.0), axis=-1, keepdims=True)
        rank = jnp.sum(jnp.where(onehots[k], before, 0.0), axis=-1, keepdims=True)
        idx_out = jnp.where(lane == k, first, idx_out)
        gate_out = jnp.where(lane == k, exps[k] / denom, gate_out)
        rank_out = jnp.where(lane == k, rank, rank_out)
    idx_ref[...] = idx_out.astype(jnp.int32)
    gate_ref[...] = gate_out
    rank_ref[...] = rank_out.astype(jnp.int32)
    carry[...] = carry[...] + jnp.sum(member, axis=0, keepdims=True)
    cnt_ref[...] = jnp.broadcast_to(carry[...], cnt_ref.shape).astype(jnp.int32)


def _post(o, w_bf16, h, gate, shift, scale, w_router, b_router, seq):
    n_tok = h.shape[0]
    tm = TM
    per_b = seq // tm
    vec = pl.BlockSpec((1, 1, D), lambda i: (i // per_b, 0, 0))
    row = pl.BlockSpec((tm, D), lambda i: (i, 0))
    small = pl.BlockSpec((tm, LANES), lambda i: (i, 0))
    r = jnp.arange(tm)
    tri = jnp.where(r[:, None] > r[None, :], 1.0, 0.0).astype(BF16)
    wr = jnp.pad(w_router, ((0, 0), (0, LANES - N_EXPERTS)))
    br = jnp.pad(b_router, (0, LANES - N_EXPERTS)).reshape(1, LANES)
    return pl.pallas_call(
        functools.partial(_post_kernel, tm=tm),
        out_shape=(jax.ShapeDtypeStruct((n_tok, D), F32),
                   jax.ShapeDtypeStruct((n_tok * SLABS, LANES), F32),
                   jax.ShapeDtypeStruct((n_tok, LANES), jnp.int32),
                   jax.ShapeDtypeStruct((n_tok, LANES), F32),
                   jax.ShapeDtypeStruct((n_tok, LANES), jnp.int32),
                   jax.ShapeDtypeStruct((SUBLANES, LANES), jnp.int32)),
        grid=(n_tok // tm,),
        in_specs=[row, pl.BlockSpec((D, D), lambda i: (0, 0)), row, vec, vec, vec,
                  pl.BlockSpec((D, LANES), lambda i: (0, 0)),
                  pl.BlockSpec((1, LANES), lambda i: (0, 0)),
                  pl.BlockSpec((tm, tm), lambda i: (0, 0))],
        out_specs=(row, pl.BlockSpec((tm * SLABS, LANES), lambda i: (i, 0)), small, small, small,
                   pl.BlockSpec((SUBLANES, LANES), lambda i: (0, 0))),
        scratch_shapes=[pltpu.VMEM((1, LANES), F32)],
        compiler_params=_params(("arbitrary",)),
    )(o, w_bf16, h, gate, shift, scale, wr, br, tri)


TOK_BITS = 14
N_DUMMY = 2
N_PARTS = 4


def _tile_copy(src, src_row, dst, dst_row, sem):
    return pltpu.make_async_copy(src.at[pl.ds(pl.multiple_of(src_row * SLABS, SLABS), SLABS)],
                                 dst.at[pl.ds(pl.multiple_of(dst_row * SLABS, SLABS), SLABS)], sem)


def _expert_kernel(be_ref, nv_ref, row_ref, xs_ref, wg_ref, bg_ref, wu_ref, bu_ref, wd_ref, bd_ref,
                   y_ref, xbuf, ybuf, xb, wg_s, wu_s, wd_s, gsem, ssem, *, n_blocks):
    j = pl.program_id(0)
    nv = nv_ref[0]
    n_rows = n_blocks * BM
    slot = j % 2
    part = BM // N_PARTS

    def gather_wait(s, p):
        pltpu.make_async_copy(xs_ref.at[pl.ds(0, part * SLABS)],
                              xbuf.at[s, pl.ds(p * part * SLABS, part * SLABS)], gsem.at[s, p]).wait()

    def scatter_wait(s):
        pltpu.make_async_copy(ybuf.at[s], y_ref.at[pl.ds(0, BM * SLABS)], ssem.at[s]).wait()

    def gather_start(block, s, rows=range(BM)):
        for i in rows:
            tok = row_ref[block * BM + i] & ((1 << TOK_BITS) - 1)
            _tile_copy(xs_ref, tok, xbuf.at[s], i, gsem.at[s, i // part]).start()

    def scatter_start(dst_of, s, rows=range(BM)):
        for i in rows:
            _tile_copy(ybuf.at[s], i, y_ref, dst_of(i), ssem.at[s]).start()

    def ffn(x):
        hg = jnp.dot(x, wg_s[...], preferred_element_type=F32) + bg_ref[...]
        hu = jnp.dot(x, wu_s[...], preferred_element_type=F32) + bu_ref[...]
        g = jnp.minimum(hg, SWIGLU_LIMIT)
        u = jnp.clip(hu, -SWIGLU_LIMIT, SWIGLU_LIMIT)
        act = g * _sigmoid(SWIGLU_ALPHA * g) * (u + 1.0)
        return jnp.dot(act.astype(BF16), wd_s[...], preferred_element_type=F32) + bd_ref[...]

    def stage_rows(s, p):
        xb[pl.ds(p * part, part), :] = jnp.concatenate(
            [xbuf[s, pl.ds(p * part * SLABS + k, part, stride=SLABS), :] for k in range(SLABS)],
            axis=1).astype(BF16)

    @pl.when(j == 0)
    def _():
        ybuf[...] = jnp.zeros_like(ybuf)
        gather_start(0, 0)
        scatter_start(lambda i: n_rows + BM + i, 0)

    @pl.when(j >= nv)
    def _():
        ybuf[0] = jnp.zeros((BM * SLABS, LANES), F32)
        first = pl.multiple_of((row_ref[j * BM] >> TOK_BITS) * SLABS, SLABS)
        tail = pltpu.make_async_copy(ybuf.at[0], y_ref.at[pl.ds(first, BM * SLABS)], ssem.at[0])
        tail.start()
        tail.wait()

    @pl.when(j < nv)
    def _():
        prev = be_ref[jnp.maximum(j - 1, 0)]

        @pl.when((j == 0) | (be_ref[j] != prev))
        def _():
            wg_s[...] = wg_ref[...].astype(BF16)
            wu_s[...] = wu_ref[...].astype(BF16)
            wd_s[...] = wd_ref[...].astype(BF16)

        next_block = jnp.minimum(j + 1, n_blocks - 1)
        prev_base = jnp.maximum(j - 1, 0) * BM
        prev_dst = lambda i: jnp.where(j > 0, row_ref[prev_base + i] >> TOK_BITS, n_rows + i)
        bounds = [BM * g // (N_PARTS - 1) for g in range(N_PARTS)]

        def block_step(cur):
            gather_wait(cur, 0)
            stage_rows(cur, 0)
            ys = []
            for p in range(N_PARTS):
                if p + 1 < N_PARTS:
                    rows = range(bounds[p], bounds[p + 1])
                    gather_start(next_block, 1 - cur, rows)
                    scatter_start(prev_dst, 1 - cur, rows)
                ys.append(ffn(xb[pl.ds(p * part, part), :]))
                if p + 1 < N_PARTS:
                    gather_wait(cur, p + 1)
                    stage_rows(cur, p + 1)
            scatter_wait(cur)
            for p in range(N_PARTS):
                for k in range(SLABS):
                    ybuf[cur, pl.ds(p * part * SLABS + k, part, stride=SLABS), :] = (
                        ys[p][:, k * LANES:(k + 1) * LANES])

            @pl.when(j == nv - 1)
            def _():
                base = j * BM
                scatter_start(lambda i: row_ref[base + i] >> TOK_BITS, cur)
                scatter_wait(cur)
                scatter_wait(1 - cur)
                for p in range(N_PARTS):
                    gather_wait(1 - cur, p)

        for cur in range(2):
            pl.when(slot == cur)(functools.partial(block_step, cur))


def _experts(block_expert, n_valid, row_words, xs, w_gate, b_gate, w_up, b_up, w_down, b_down, layer):
    n_blocks = block_expert.shape[0]
    w_spec = pl.BlockSpec((None, None, D, D), lambda j, be, nv, rw: (layer, be[j], 0, 0))
    b_spec = pl.BlockSpec((None, None, 1, D), lambda j, be, nv, rw: (layer, be[j], 0, 0))
    r4 = lambda b: b.reshape(b.shape[0], N_EXPERTS, 1, D)
    hbm = pl.BlockSpec(memory_space=pl.ANY)
    return pl.pallas_call(
        functools.partial(_expert_kernel, n_blocks=n_blocks),
        out_shape=jax.ShapeDtypeStruct(((n_blocks + N_DUMMY) * BM * SLABS, LANES), F32),
        grid_spec=pltpu.PrefetchScalarGridSpec(
            num_scalar_prefetch=3,
            grid=(n_blocks,),
            in_specs=[hbm, w_spec, b_spec, w_spec, b_spec, w_spec, b_spec],
            out_specs=hbm,
            scratch_shapes=[pltpu.VMEM((2, BM * SLABS, LANES), F32),
                            pltpu.VMEM((2, BM * SLABS, LANES), F32),
                            pltpu.VMEM((BM, D), BF16),
                            pltpu.VMEM((D, D), BF16), pltpu.VMEM((D, D), BF16), pltpu.VMEM((D, D), BF16),
                            pltpu.SemaphoreType.DMA((2, N_PARTS)), pltpu.SemaphoreType.DMA((2,))]),
        compiler_params=_params(("arbitrary",)),
    )(block_expert, n_valid, row_words, xs, w_gate, r4(b_gate), w_up, r4(b_up), w_down, r4(b_down))


def _combine_kernel(y_ref, h_ref, gate_ref, g2_ref, o_ref, *, td):
    gates = gate_ref[...]
    acc = jnp.zeros((td, D), F32)
    for k in range(TOP_K):
        yk = jnp.concatenate(
            [y_ref[pl.ds(k * SLABS + s, td, stride=TOP_K * SLABS), :] for s in range(SLABS)], axis=1)
        acc = acc + gates[:, k:k + 1] * yk
    o_ref[...] = h_ref[...] + g2_ref[0] * acc


def _combine(y_tok, h1, gates, g2, seq):
    n_tok = h1.shape[0]
    td = TD
    per_b = seq // td
    return pl.pallas_call(
        functools.partial(_combine_kernel, td=td),
        out_shape=jax.ShapeDtypeStruct((n_tok, D), F32),
        grid=(n_tok // td,),
        in_specs=[pl.BlockSpec((td * TOP_K * SLABS, LANES), lambda i: (i, 0)),
                  pl.BlockSpec((td, D), lambda i: (i, 0)),
                  pl.BlockSpec((td, LANES), lambda i: (i, 0)),
                  pl.BlockSpec((1, 1, D), lambda i: (i // per_b, 0, 0))],
        out_specs=pl.BlockSpec((td, D), lambda i: (i, 0)),
        compiler_params=_params(("arbitrary",)),
    )(y_tok, h1, gates, g2)


def _moe(o, w_out_bf16, h, g1, sh2, sc2, g2, w_router, b_router, w_gate, b_gate, w_up, b_up,
         w_down, b_down, layer, seq):
    n_tok = h.shape[0]
    n_pairs = n_tok * TOP_K
    h1, xs, idx, gates, rank, cnt = _post(o, w_out_bf16, h, g1, sh2, sc2, w_router, b_router, seq)
    counts = cnt[0, :N_EXPERTS]
    padded = (counts + BM - 1) // BM * BM
    pend = jnp.cumsum(padded)
    pstart = pend - padded
    pos_flat = (pstart[idx[:, :TOP_K]] + rank[:, :TOP_K]).reshape(-1)
    n_blocks = (n_pairs + N_EXPERTS * (BM - 1)) // BM
    n_rows = n_blocks * BM
    block_start = jnp.arange(n_blocks, dtype=jnp.int32) * BM
    block_expert = jnp.minimum(
        jnp.sum((pend[None, :] <= block_start[:, None]).astype(jnp.int32), axis=1), N_EXPERTS - 1)
    n_valid = (pend[-1:] // BM).astype(jnp.int32)
    row_pair = jnp.full((n_rows,), -1, jnp.int32).at[pos_flat].set(
        jnp.arange(n_pairs, dtype=jnp.int32), unique_indices=True)
    is_pad = row_pair < 0
    spare = n_pairs + jnp.cumsum(is_pad.astype(jnp.int32)) - 1
    row_words = jnp.where(is_pad, spare << TOK_BITS, (row_pair << TOK_BITS) | (row_pair // TOP_K))
    y_tok = _experts(block_expert, n_valid, row_words, xs, w_gate, b_gate, w_up, b_up, w_down, b_down,
                     layer)
    return _combine(y_tok, h1, gates, g2, seq)


def _group_rms(y, ind_ref, indt_ref):
    ss = jnp.dot((y * y).astype(BF16), ind_ref[...], preferred_element_type=F32)
    r = lax.rsqrt(ss * (1.0 / HALF) + EPS)
    r_hi, r_lo = _split(r)
    d = functools.partial(jnp.dot, preferred_element_type=F32)
    return y * (d(r_hi, indt_ref[...]) + d(r_lo, indt_ref[...]))


def _qkv_kernel(h_ref, ksh_ref, ksc_ref, qsh_ref, qsc_ref, wk_ref, wv_ref, wq_ref, kn_ref, qn_ref,
                ind_ref, indt_ref, q_ref, k_ref, v_ref):
    h = h_ref[...]
    hkv = _modulate(h, ksh_ref[0], ksc_ref[0]).astype(BF16)
    xn = _modulate(h, qsh_ref[0], qsc_ref[0]).astype(BF16)
    k = jnp.dot(hkv, wk_ref[...], preferred_element_type=F32)
    v = jnp.dot(hkv, wv_ref[...], preferred_element_type=F32)
    q = jnp.dot(xn, wq_ref[...], preferred_element_type=F32)
    k_ref[...] = (_group_rms(k, ind_ref, indt_ref) * kn_ref[...]).astype(BF16)
    q_ref[...] = (_group_rms(q, ind_ref, indt_ref) * qn_ref[...] * (HALF ** -0.5)).astype(BF16)
    v_ref[...] = v.astype(BF16)


def _qkv(h, kv_shift, kv_scale, q_shift, q_scale, w_k, w_v, w_q, k_norm, q_norm, seq):
    n_tok = h.shape[0]
    per_b = seq // TM
    vec = pl.BlockSpec((1, 1, D), lambda i: (i // per_b, 0, 0))
    row = pl.BlockSpec((TM, D), lambda i: (i, 0))
    full = lambda shape: pl.BlockSpec(shape, lambda i: (0, 0))
    groups = jnp.arange(D) // HALF
    ind = (groups[:, None] == jnp.arange(LANES)[None, :]).astype(BF16)
    out = jax.ShapeDtypeStruct((n_tok, D), BF16)
    return pl.pallas_call(
        _qkv_kernel,
        out_shape=(out, out, out),
        grid=(n_tok // TM,),
        in_specs=[row, vec, vec, vec, vec, full((D, D)), full((D, D)), full((D, D)),
                  full((1, D)), full((1, D)), full((D, LANES)), full((LANES, D))],
        out_specs=(row, row, row),
        compiler_params=_params(("arbitrary",)),
    )(h, kv_shift, kv_scale, q_shift, q_scale, w_k.astype(BF16), w_v.astype(BF16), w_q.astype(BF16),
      jnp.tile(k_norm, D // HALF).reshape(1, D), jnp.tile(q_norm, D // HALF).reshape(1, D),
      ind, ind.T)


def _attn_kernel(q_ref, k_ref, v_ref, lam_ref, sub_ref, o_ref, *, tq, lambda_init):
    qi = pl.program_id(2)
    half = tq // 2
    lane = lax.broadcasted_iota(jnp.int32, (half, HEAD), 1)
    ones = jnp.ones((tq, HEAD), BF16)
    shift = CHUNK.bit_length() - 1
    diag_ok = ((lax.broadcasted_iota(jnp.int32, (half, half), 1) >> shift)
               <= (lax.broadcasted_iota(jnp.int32, (half, half), 0) >> shift))

    units = [(slab, m) for slab in range(2) for m in range(2)]
    qs = {}
    for slab, m in units:
        q = q_ref[pl.ds(slab * half, half), :]
        qs[slab, m] = jnp.where((lane < HALF) if m == 0 else (lane >= HALF), q, jnp.zeros_like(q))

    def scores(u, k):
        return lax.dot_general(qs[u], k, (((1,), (1,)), ((), ())), preferred_element_type=F32)

    def update(s, v1, carry):
        m_new = [jnp.maximum(carry[i][0], jnp.max(s[i], axis=-1, keepdims=True)) for i in range(4)]
        alpha = [jnp.exp(carry[i][0] - m_new[i]) for i in range(4)]
        p = [jnp.exp(s[i] - m_new[i]).astype(BF16) for i in range(4)]
        acc = [alpha[i] * carry[i][1] + jnp.dot(p[i], v1[i], preferred_element_type=F32) for i in range(4)]
        return tuple((m_new[i], acc[i]) for i in range(4))

    def kv_tile(kj):
        start = pl.multiple_of(kj * tq, tq)
        k = k_ref[pl.ds(start, tq), :]
        v1 = jnp.concatenate([v_ref[pl.ds(start, tq), :], ones], axis=1)
        return k, v1

    def full_tile(kj, carry):
        k, v1 = kv_tile(kj)
        return update([scores(u, k) for u in units], [v1] * 4, carry)

    def diag_tile(carry):
        k, v1 = kv_tile(qi)
        s, vs = [], []
        for u in units:
            if u[0] == 0:
                s.append(jnp.where(diag_ok, scores(u, k[:half]), NEG))
                vs.append(v1[:half])
            else:
                full = scores(u, k)
                s.append(jnp.concatenate([full[:, :half], jnp.where(diag_ok, full[:, half:], NEG)], axis=1))
                vs.append(v1)
        return update(s, vs, carry)

    init = tuple((jnp.full((half, 1), NEG, F32), jnp.zeros((half, 2 * HEAD), F32)) for _ in units)
    carry = diag_tile(lax.fori_loop(0, qi, full_tile, init))

    lp = lam_ref[...]
    lam = (jnp.exp(jnp.sum(lp[0:1] * lp[1:2], axis=-1, keepdims=True))
           - jnp.exp(jnp.sum(lp[2:3] * lp[3:4], axis=-1, keepdims=True)) + lambda_init)
    for slab in range(2):
        acc0, acc1 = carry[2 * slab][1], carry[2 * slab + 1][1]
        o = acc0[:, :HEAD] / acc0[:, HEAD:HEAD + 1] - lam * (acc1[:, :HEAD] / acc1[:, HEAD:HEAD + 1])
        o = o * lax.rsqrt(jnp.mean(o * o, axis=-1, keepdims=True) + EPS) * sub_ref[...] * (1.0 - lambda_init)
        o_ref[pl.ds(slab * half, half), :] = o.astype(o_ref.dtype)


def _attention(q, k, v, lam_params, subln, bsz, seq, lambda_init):
    tq = TQ
    nq = seq // tq
    n_tok = bsz * seq
    k3 = k.reshape(bsz, seq, D)
    v3 = v.reshape(bsz, seq, D)
    kv_spec = pl.BlockSpec((None, seq, HEAD), lambda b, h, i: (b, 0, h))
    return pl.pallas_call(
        functools.partial(_attn_kernel, tq=tq, lambda_init=lambda_init),
        out_shape=jax.ShapeDtypeStruct((n_tok, D), BF16),
        grid=(bsz, N_HEADS, nq),
        in_specs=[pl.BlockSpec((tq, HEAD), lambda b, h, i: (b * nq + i, h)), kv_spec, kv_spec,
                  pl.BlockSpec((SUBLANES, LANES), lambda b, h, i: (0, 0)),
                  pl.BlockSpec((1, HEAD), lambda b, h, i: (0, 0))],
        out_specs=pl.BlockSpec((tq, HEAD), lambda b, h, i: (b * nq + i, h)),
        compiler_params=_params(("arbitrary", "arbitrary", "arbitrary")),
    )(q, k3, v3, lam_params, subln.reshape(1, HEAD))


def kernel(x, c, ada_w, ada_b, a_w_in, a_conv, a_log, a_dt_bias, a_norm, a_w_out, kv_ada_w, kv_ada_b,
           b_w_k, b_w_v, b_k_norm, b_w_q, b_q_norm, b_lam_q1, b_lam_k1, b_lam_q2, b_lam_k2, b_subln,
           b_w_out, moe_w_router, moe_b_router, moe_w_gate, moe_b_gate, moe_w_up, moe_b_up,
           moe_w_down, moe_b_down):
    bsz, seq, _ = x.shape
    h = x.reshape(bsz * seq, D)

    def six(mod):
        return [mod[:, i * D:(i + 1) * D].reshape(bsz, 1, D) for i in range(6)]

    moe_args = (moe_w_router, moe_b_router, moe_w_gate, moe_b_gate, moe_w_up, moe_b_up,
                moe_w_down, moe_b_down)

    def moe(o, w_out, h, g1, sh2, sc2, g2, layer):
        return _moe(o, w_out.astype(BF16), h, g1, sh2, sc2, g2, moe_w_router[layer],
                    moe_b_router[layer], *moe_args[2:], layer, seq)

    sh1, sc1, g1, sh2, sc2, g2 = six(_mod_vectors(c, ada_w, ada_b, 0))
    n_in = a_w_in.shape[-1]
    w_in = jnp.pad(a_w_in[0], ((0, 0), (0, IN_COLS - n_in))).astype(BF16)
    proj = _inproj(h, sh1, sc1, w_in, seq)
    dec = jnp.zeros((SUBLANES, LANES), F32)
    dec = dec.at[0, N_HEADS:2 * N_HEADS].set(a_log[0]).at[1, N_HEADS:2 * N_HEADS].set(a_dt_bias[0])
    o = _delta(proj, a_conv[0], dec, a_norm[0].reshape(1, HEAD), bsz, seq)
    h = moe(o, a_w_out[0], h, g1, sh2, sc2, g2, 0)

    kv_mod = _mod_vectors(c, kv_ada_w, kv_ada_b, None)
    kv_shift = kv_mod[:, :D].reshape(bsz, 1, D)
    kv_scale = kv_mod[:, D:].reshape(bsz, 1, D)
    sh1, sc1, g1, sh2, sc2, g2 = six(_mod_vectors(c, ada_w, ada_b, 1))
    q, k, v = _qkv(h, kv_shift, kv_scale, sh1, sc1, b_w_k, b_w_v, b_w_q[0], b_k_norm, b_q_norm[0], seq)

    lambda_init = 0.8 - 0.6 * math.exp(-0.3 * 1)
    lam_params = jnp.zeros((SUBLANES, LANES), F32)
    for r, p in enumerate((b_lam_q1, b_lam_k1, b_lam_q2, b_lam_k2)):
        lam_params = lam_params.at[r, :HALF].set(p[0])
    o = _attention(q, k, v, lam_params, b_subln[0], bsz, seq, lambda_init)
    h = moe(o, b_w_out[0], h, g1, sh2, sc2, g2, 1)
    return h.reshape(bsz, seq, D)
```

```python
import functools
import math

import jax
import jax.numpy as jnp
from jax import lax
from jax.experimental import pallas as pl
from jax.experimental.pallas import tpu as pltpu

F32 = jnp.float32
BF16 = jnp.bfloat16

D = 1024
CHUNK = 64
N_HEADS = 8
HEAD = 128
HALF = 64
CONV_W = 4
N_EXPERTS = 32
TOP_K = 4
SWIGLU_LIMIT = 7.0
SWIGLU_ALPHA = 1.702
EPS = 1e-6
NEG = -1e30

LANES = 128
SUBLANES = 8
SLABS = D // LANES
VMEM_LIMIT = 56 * 1024 * 1024

TM = 512
T_DELTA = 256
TQ = 512
BM = 512
TD = 128
IN_COLS = 4224


def _dot(a, b):
    return jnp.dot(a.astype(BF16), b.astype(BF16), preferred_element_type=F32)


def _dot_nt(a, b):
    return lax.dot_general(a.astype(BF16), b.astype(BF16), (((1,), (1,)), ((), ())),
                           preferred_element_type=F32)


def _split(a):
    hi = a.astype(BF16)
    lo = (a - hi.astype(F32)).astype(BF16)
    return hi, lo


def _dot3(a, b):
    a_hi, a_lo = _split(a)
    b_hi, b_lo = _split(b)
    d = functools.partial(jnp.dot, preferred_element_type=F32)
    return d(a_hi, b_hi) + (d(a_hi, b_lo) + d(a_lo, b_hi))


def _sigmoid(x):
    return 1.0 / (1.0 + jnp.exp(-x))


def _silu(x):
    return x * _sigmoid(x)


def _modulate(h, shift, scale):
    ms = jnp.mean(h * h, axis=-1, keepdims=True)
    return h * lax.rsqrt(ms + EPS) * (1.0 + scale) + shift


def _params(sem):
    return pltpu.CompilerParams(dimension_semantics=sem, vmem_limit_bytes=VMEM_LIMIT)


def _mod_kernel(c_ref, w_ref, b_ref, o_ref):
    o_ref[...] = _dot3(_silu(c_ref[...]), w_ref[...]) + b_ref[...]


def _mod_vectors(c, w, b, layer):
    n = w.shape[-1]
    tn = 1024
    if layer is None:
        w_spec = pl.BlockSpec((D, tn), lambda j: (0, j))
        b2 = b.reshape(1, n)
        b_spec = pl.BlockSpec((1, tn), lambda j: (0, j))
    else:
        w_spec = pl.BlockSpec((None, D, tn), lambda j: (layer, 0, j))
        b2 = b.reshape(b.shape[0], 1, n)
        b_spec = pl.BlockSpec((None, 1, tn), lambda j: (layer, 0, j))
    bsz = c.shape[0]
    return pl.pallas_call(
        _mod_kernel,
        out_shape=jax.ShapeDtypeStruct((bsz, n), F32),
        grid=(n // tn,),
        in_specs=[pl.BlockSpec((bsz, D), lambda j: (0, 0)), w_spec, b_spec],
        out_specs=pl.BlockSpec((bsz, tn), lambda j: (0, j)),
        compiler_params=_params(("arbitrary",)),
    )(c, w, b2)


def _inproj_kernel(h_ref, sh_ref, sc_ref, w_ref, o_ref):
    xn = _modulate(h_ref[...], sh_ref[0], sc_ref[0])
    o_ref[...] = jnp.dot(xn.astype(BF16), w_ref[...], preferred_element_type=F32)


def _inproj(h, shift, scale, w_bf16, seq):
    n_tok = h.shape[0]
    per_b = seq // TM
    vec = pl.BlockSpec((1, 1, D), lambda i: (i // per_b, 0, 0))
    return pl.pallas_call(
        _inproj_kernel,
        out_shape=jax.ShapeDtypeStruct((n_tok, IN_COLS), F32),
        grid=(n_tok // TM,),
        in_specs=[pl.BlockSpec((TM, D), lambda i: (i, 0)), vec, vec,
                  pl.BlockSpec((D, IN_COLS), lambda i: (0, 0))],
        out_specs=pl.BlockSpec((TM, IN_COLS), lambda i: (i, 0)),
        compiler_params=_params(("arbitrary",)),
    )(h, shift, scale, w_bf16)


N_QKV = 3 * D


def _delta_kernel(qkv_ref, z_ref, sm_ref, cw_ref, dec_ref, nw_ref, tri_ref, o_ref, pad, state, *, t):
    @pl.when(pl.program_id(1) == 0)
    def _():
        pad[pl.ds(0, SUBLANES), :] = jnp.zeros((SUBLANES, N_QKV), F32)
        state[...] = jnp.zeros_like(state)

    pad[pl.ds(SUBLANES, t), :] = qkv_ref[...]

    sm = sm_ref[...]
    sig = _sigmoid(sm)
    dec = dec_ref[...]
    xs = sm + dec[1:2]
    softplus = jnp.maximum(xs, 0.0) + jnp.log(1.0 + jnp.exp(-jnp.abs(xs)))
    g_raw = -jnp.exp(dec[0:1]) * softplus
    g_cum = _dot3(tri_ref[...], g_raw)
    g_t = g_cum.T
    lane = lax.broadcasted_iota(jnp.int32, (t, LANES), 1)

    ri = lax.broadcasted_iota(jnp.int32, (CHUNK, CHUNK), 0)
    ci = lax.broadcasted_iota(jnp.int32, (CHUNK, CHUNK), 1)
    incl = ri >= ci
    strict = ri > ci
    eye = jnp.where(ri == ci, 1.0, 0.0)
    level_masks = []
    size = 2
    while size < CHUNK:
        shift = size.bit_length() - 1
        level_masks.append(((ri >> (shift + 1)) == (ci >> (shift + 1))) & ((ri >> shift) != (ci >> shift)))
        size *= 2
    nw = nw_ref[...]

    def conv_silu(col):
        cols = slice(col * HEAD, (col + 1) * HEAD)
        w = cw_ref[:, cols]
        y = w[3:4] * pad[pl.ds(SUBLANES, t), cols]
        for k in range(1, CONV_W):
            y = y + w[3 - k:4 - k] * pad[pl.ds(SUBLANES - k, t), cols]
        return _silu(y)

    n_chunks = t // CHUNK
    units = [(hd, c) for hd in range(N_HEADS) for c in range(n_chunks)]
    qs, ks, kbs, vbs, gcs, egs, grs = {}, {}, {}, {}, {}, {}, {}
    for hd in range(N_HEADS):
        q = conv_silu(hd)
        k = conv_silu(N_HEADS + hd)
        v = conv_silu(2 * N_HEADS + hd)
        q = q * lax.rsqrt(jnp.sum(q * q, axis=-1, keepdims=True) + EPS) * (HEAD ** -0.5)
        k = k * lax.rsqrt(jnp.sum(k * k, axis=-1, keepdims=True) + EPS)
        beta = jnp.sum(jnp.where(lane == hd, sig, 0.0), axis=-1, keepdims=True)
        g_col = jnp.sum(jnp.where(lane == hd + N_HEADS, g_cum, 0.0), axis=-1, keepdims=True)
        g_row = g_t[N_HEADS + hd:N_HEADS + hd + 1, :]
        kb = k * beta
        vb = v * beta
        eg = jnp.exp(g_col)
        for c in range(n_chunks):
            rows = slice(c * CHUNK, (c + 1) * CHUNK)
            u = (hd, c)
            qs[u], ks[u], kbs[u], vbs[u], gcs[u], egs[u] = q[rows], k[rows], kb[rows], vb[rows], g_col[rows], eg[rows]
            grs[u] = g_row[:, c * CHUNK:(c + 1) * CHUNK]

    decay = {u: jnp.where(incl, jnp.exp(jnp.where(incl, gcs[u] - grs[u], 0.0)), 0.0) for u in units}
    both = {u: _dot_nt(jnp.concatenate([kbs[u], qs[u]], axis=0), ks[u]) for u in units}
    l_mat = {u: jnp.where(strict, both[u][:CHUNK] * decay[u], 0.0) for u in units}
    qk = {u: both[u][CHUNK:] * decay[u] for u in units}
    t_inv = {u: eye - jnp.where((ri ^ ci) == 1, l_mat[u], 0.0) for u in units}
    for pair in level_masks:
        xc = {u: _dot(t_inv[u], jnp.where(pair, l_mat[u], 0.0)) for u in units}
        t_inv = {u: t_inv[u] - _dot(xc[u], t_inv[u]) for u in units}
    uw = {u: _dot(t_inv[u], jnp.concatenate([vbs[u], kbs[u] * egs[u]], axis=1)) for u in units}
    g_last = {u: gcs[u][CHUNK - 1:CHUNK] for u in units}
    k_dec_t = {u: (ks[u] * jnp.exp(g_last[u] - gcs[u])).T for u in units}
    q_dec = {u: qs[u] * egs[u] for u in units}

    heads = range(N_HEADS)
    s = [state[hd] for hd in heads]
    for c in range(n_chunks):
        ws = [_dot(jnp.concatenate([uw[hd, c][:, HEAD:], q_dec[hd, c]], axis=0), s[hd]) for hd in heads]
        v_new = [uw[hd, c][:, :HEAD] - ws[hd][:CHUNK] for hd in heads]
        o = [ws[hd][CHUNK:] + _dot(qk[hd, c], v_new[hd]) for hd in heads]
        s = [s[hd] * jnp.exp(g_last[hd, c]) + _dot(k_dec_t[hd, c], v_new[hd]) for hd in heads]
        for hd in heads:
            zc = z_ref[pl.ds(c * CHUNK, CHUNK), hd * HEAD:(hd + 1) * HEAD]
            on = o[hd] * lax.rsqrt(jnp.mean(o[hd] * o[hd], axis=-1, keepdims=True) + EPS) * nw * _silu(zc)
            o_ref[pl.ds(c * CHUNK, CHUNK), hd * HEAD:(hd + 1) * HEAD] = on.astype(o_ref.dtype)
    for hd in heads:
        state[hd] = s[hd]

    pad[pl.ds(0, SUBLANES), :] = pad[pl.ds(t, SUBLANES), :]


def _chunk_tri(t):
    r = jnp.arange(t)
    same = (r[:, None] // CHUNK) == (r[None, :] // CHUNK)
    return jnp.where(same & (r[:, None] >= r[None, :]), 1.0, 0.0).astype(F32)


def _delta(proj, conv_w, dec, norm_w, bsz, seq):
    t = T_DELTA
    nt = seq // t
    n_tok = bsz * seq
    const = lambda shape: pl.BlockSpec(shape, lambda b, j: (0, 0))
    return pl.pallas_call(
        functools.partial(_delta_kernel, t=t),
        out_shape=jax.ShapeDtypeStruct((n_tok, D), BF16),
        grid=(bsz, nt),
        in_specs=[pl.BlockSpec((t, N_QKV), lambda b, j: (b * nt + j, 0)),
                  pl.BlockSpec((t, D), lambda b, j: (b * nt + j, N_QKV // D)),
                  pl.BlockSpec((t, LANES), lambda b, j: (b * nt + j, (N_QKV + D) // LANES)),
                  const((CONV_W, N_QKV)), const((SUBLANES, LANES)), const((1, HEAD)), const((t, t))],
        out_specs=pl.BlockSpec((t, D), lambda b, j: (b * nt + j, 0)),
        scratch_shapes=[pltpu.VMEM((t + SUBLANES, N_QKV), F32),
                        pltpu.VMEM((N_HEADS, HEAD, HEAD), F32)],
        compiler_params=_params(("arbitrary", "arbitrary")),
    )(proj, proj, proj, conv_w, dec, norm_w, _chunk_tri(t))


def _post_kernel(o_ref, w_ref, h_ref, g_ref, sh_ref, sc_ref, wr_ref, br_ref, tri_ref,
                 h1_ref, xs_ref, idx_ref, gate_ref, rank_ref, cnt_ref, carry, *, tm):
    @pl.when(pl.program_id(0) == 0)
    def _():
        carry[...] = jnp.zeros_like(carry)

    y = jnp.dot(o_ref[...], w_ref[...], preferred_element_type=F32)
    h1 = h_ref[...] + g_ref[0] * y
    h1_ref[...] = h1
    xn = _modulate(h1, sh_ref[0], sc_ref[0])
    for s in range(SLABS):
        xs_ref[pl.ds(s, tm, stride=SLABS), :] = xn[:, s * LANES:(s + 1) * LANES]

    lane = lax.broadcasted_iota(jnp.int32, (tm, LANES), 1)
    lane_f = lane.astype(F32)
    logits = jnp.where(lane < N_EXPERTS, _dot3(xn, wr_ref[...]) + br_ref[...], NEG)
    vals, onehots = [], []
    for _ in range(TOP_K):
        m = jnp.max(logits, axis=-1, keepdims=True)
        first = jnp.min(jnp.where(logits == m, lane_f, float(LANES)), axis=-1, keepdims=True)
        hit = lane_f == first
        vals.append(m)
        onehots.append(hit)
        logits = jnp.where(hit, 2.0 * NEG, logits)
    exps = [jnp.exp(v - vals[0]) for v in vals]
    denom = exps[0] + exps[1] + exps[2] + exps[3]

    member = jnp.zeros((tm, LANES), F32)
    for hit in onehots:
        member = jnp.where(hit, 1.0, member)
    before = jnp.dot(tri_ref[...], member.astype(BF16), preferred_element_type=F32) + carry[...]
    idx_out = jnp.zeros((tm, LANES), F32)
    gate_out = jnp.zeros((tm, LANES), F32)
    rank_out = jnp.zeros((tm, LANES), F32)
    for k in range(TOP_K):
        first = jnp.sum(jnp.where(onehots[k], lane_f, 0.0), axis=-1, keepdims=True)
        rank = jnp.sum(jnp.where(onehots[k], before, 0.0), axis=-1, keepdims=True)
        idx_out = jnp.where(lane == k, first, idx_out)
        gate_out = jnp.where(lane == k, exps[k] / denom, gate_out)
        rank_out = jnp.where(lane == k, rank, rank_out)
    idx_ref[...] = idx_out.astype(jnp.int32)
    gate_ref[...] = gate_out
    rank_ref[...] = rank_out.astype(jnp.int32)
    carry[...] = carry[...] + jnp.sum(member, axis=0, keepdims=True)
    cnt_ref[...] = jnp.broadcast_to(carry[...], cnt_ref.shape).astype(jnp.int32)


def _post(o, w_bf16, h, gate, shift, scale, w_router, b_router, seq):
    n_tok = h.shape[0]
    tm = TM
    per_b = seq // tm
    vec = pl.BlockSpec((1, 1, D), lambda i: (i // per_b, 0, 0))
    row = pl.BlockSpec((tm, D), lambda i: (i, 0))
    small = pl.BlockSpec((tm, LANES), lambda i: (i, 0))
    r = jnp.arange(tm)
    tri = jnp.where(r[:, None] > r[None, :], 1.0, 0.0).astype(BF16)
    wr = jnp.pad(w_router, ((0, 0), (0, LANES - N_EXPERTS)))
    br = jnp.pad(b_router, (0, LANES - N_EXPERTS)).reshape(1, LANES)
    return pl.pallas_call(
        functools.partial(_post_kernel, tm=tm),
        out_shape=(jax.ShapeDtypeStruct((n_tok, D), F32),
                   jax.ShapeDtypeStruct((n_tok * SLABS, LANES), F32),
                   jax.ShapeDtypeStruct((n_tok, LANES), jnp.int32),
                   jax.ShapeDtypeStruct((n_tok, LANES), F32),
                   jax.ShapeDtypeStruct((n_tok, LANES), jnp.int32),
                   jax.ShapeDtypeStruct((SUBLANES, LANES), jnp.int32)),
        grid=(n_tok // tm,),
        in_specs=[row, pl.BlockSpec((D, D), lambda i: (0, 0)), row, vec, vec, vec,
                  pl.BlockSpec((D, LANES), lambda i: (0, 0)),
                  pl.BlockSpec((1, LANES), lambda i: (0, 0)),
                  pl.BlockSpec((tm, tm), lambda i: (0, 0))],
        out_specs=(row, pl.BlockSpec((tm * SLABS, LANES), lambda i: (i, 0)), small, small, small,
                   pl.BlockSpec((SUBLANES, LANES), lambda i: (0, 0))),
        scratch_shapes=[pltpu.VMEM((1, LANES), F32)],
        compiler_params=_params(("arbitrary",)),
    )(o, w_bf16, h, gate, shift, scale, wr, br, tri)


def _row_copy(src, src_row, dst, dst_row, sem):
    return pltpu.make_async_copy(src.at[pl.ds(pl.multiple_of(src_row * SLABS, SLABS), SLABS)],
                                 dst.at[pl.ds(pl.multiple_of(dst_row * SLABS, SLABS), SLABS)], sem)


def _dispatch_kernel(pos_ref, pend_ref, cnt_ref, x_ref, out_ref, buf, zbuf, sems, zsem, *, td, n_tiles,
                     n_blocks):
    i = pl.program_id(0)
    slot = i % 2

    def wait_slot(s):
        for _ in range(TOP_K):
            pltpu.make_async_copy(buf.at[s], out_ref.at[pl.ds(0, td * SLABS)], sems.at[s]).wait()

    @pl.when(i == 0)
    def _():
        zbuf[...] = jnp.zeros_like(zbuf)

        def zero_copy(e):
            last = pl.multiple_of((pend_ref[e] - BM) * SLABS, BM * SLABS)
            return pltpu.make_async_copy(zbuf, out_ref.at[pl.ds(last, BM * SLABS)], zsem)

        def tail_copy(j):
            start = pl.multiple_of(j * (BM * SLABS), BM * SLABS)
            return pltpu.make_async_copy(zbuf, out_ref.at[pl.ds(start, BM * SLABS)], zsem)

        def tail_start(j, c):
            tail_copy(j).start()
            return c

        def tail_wait(j, c):
            tail_copy(j).wait()
            return c

        n_valid = pend_ref[N_EXPERTS - 1] // BM
        for e in range(N_EXPERTS):
            @pl.when(cnt_ref[e] > 0)
            def _():
                zero_copy(e).start()
        lax.fori_loop(n_valid, n_blocks, tail_start, 0)
        for e in range(N_EXPERTS):
            @pl.when(cnt_ref[e] > 0)
            def _():
                zero_copy(e).wait()
        lax.fori_loop(n_valid, n_blocks, tail_wait, 0)

    @pl.when(i >= 2)
    def _():
        wait_slot(slot)

    buf[slot] = x_ref[...]

    def body(t, carry):
        for k in range(TOP_K):
            p = pos_ref[(i * td + t) * TOP_K + k]
            _row_copy(buf.at[slot], t, out_ref, p, sems.at[slot]).start()
        return carry

    lax.fori_loop(0, td, body, 0)

    @pl.when(i == n_tiles - 1)
    def _():
        wait_slot(slot)
        if n_tiles > 1:
            wait_slot(1 - slot)


def _dispatch(pos_flat, pend, counts, xs, n_rows):
    n_tok = xs.shape[0] // SLABS
    td = TD
    n_tiles = n_tok // td
    return pl.pallas_call(
        functools.partial(_dispatch_kernel, td=td, n_tiles=n_tiles, n_blocks=n_rows // BM),
        out_shape=jax.ShapeDtypeStruct((n_rows * SLABS, LANES), F32),
        grid_spec=pltpu.PrefetchScalarGridSpec(
            num_scalar_prefetch=3,
            grid=(n_tiles,),
            in_specs=[pl.BlockSpec((td * SLABS, LANES), lambda i, *_: (i, 0))],
            out_specs=pl.BlockSpec(memory_space=pl.ANY),
            scratch_shapes=[pltpu.VMEM((2, td * SLABS, LANES), F32),
                            pltpu.VMEM((BM * SLABS, LANES), F32),
                            pltpu.SemaphoreType.DMA((2,)),
                            pltpu.SemaphoreType.DMA(())]),
        compiler_params=_params(("arbitrary",)),
    )(pos_flat, pend, counts, xs)


def _expert_kernel(be_ref, nv_ref, x_ref, wg_ref, bg_ref, wu_ref, bu_ref, wd_ref, bd_ref, y_ref,
                   wg_s, wu_s, wd_s):
    j = pl.program_id(0)

    @pl.when(j >= nv_ref[0])
    def _():
        y_ref[...] = jnp.zeros_like(y_ref)

    @pl.when(j < nv_ref[0])
    def _():
        prev = be_ref[jnp.maximum(j - 1, 0)]

        @pl.when((j == 0) | (be_ref[j] != prev))
        def _():
            wg_s[...] = wg_ref[...].astype(BF16)
            wu_s[...] = wu_ref[...].astype(BF16)
            wd_s[...] = wd_ref[...].astype(BF16)

        x = jnp.concatenate([x_ref[pl.ds(s, BM, stride=SLABS), :] for s in range(SLABS)],
                            axis=1).astype(BF16)
        hg = jnp.dot(x, wg_s[...], preferred_element_type=F32) + bg_ref[...]
        hu = jnp.dot(x, wu_s[...], preferred_element_type=F32) + bu_ref[...]
        g = jnp.minimum(hg, SWIGLU_LIMIT)
        u = jnp.clip(hu, -SWIGLU_LIMIT, SWIGLU_LIMIT)
        act = g * _sigmoid(SWIGLU_ALPHA * g) * (u + 1.0)
        y = jnp.dot(act.astype(BF16), wd_s[...], preferred_element_type=F32) + bd_ref[...]
        for s in range(SLABS):
            y_ref[pl.ds(s, BM, stride=SLABS), :] = y[:, s * LANES:(s + 1) * LANES]


def _experts(block_expert, n_valid, x_sorted, w_gate, b_gate, w_up, b_up, w_down, b_down, layer):
    n_blocks = block_expert.shape[0]

    def blk(j, be, nv):
        return (jnp.minimum(j, nv[0] - 1), 0)

    w_spec = pl.BlockSpec((None, None, D, D), lambda j, be, nv: (layer, be[j], 0, 0))
    b_spec = pl.BlockSpec((None, None, 1, D), lambda j, be, nv: (layer, be[j], 0, 0))
    r4 = lambda b: b.reshape(b.shape[0], N_EXPERTS, 1, D)
    return pl.pallas_call(
        _expert_kernel,
        out_shape=jax.ShapeDtypeStruct(x_sorted.shape, F32),
        grid_spec=pltpu.PrefetchScalarGridSpec(
            num_scalar_prefetch=2,
            grid=(n_blocks,),
            in_specs=[pl.BlockSpec((BM * SLABS, LANES), blk),
                      w_spec, b_spec, w_spec, b_spec, w_spec, b_spec],
            out_specs=pl.BlockSpec((BM * SLABS, LANES), lambda j, be, nv: (j, 0)),
            scratch_shapes=[pltpu.VMEM((D, D), BF16)] * 3),
        compiler_params=_params(("arbitrary",)),
    )(block_expert, n_valid, x_sorted, w_gate, r4(b_gate), w_up, r4(b_up), w_down, r4(b_down))


def _combine_kernel(pos_ref, y_ref, h_ref, gate_ref, g2_ref, o_ref, buf, sems, *, td, n_tiles):
    i = pl.program_id(0)
    slot = i % 2

    def issue(tile, s):
        def body(t, carry):
            for k in range(TOP_K):
                p = pos_ref[(tile * td + t) * TOP_K + k]
                _row_copy(y_ref, p, buf.at[s, k], t, sems.at[s]).start()
            return carry

        lax.fori_loop(0, td, body, 0)

    @pl.when(i == 0)
    def _():
        issue(0, 0)

    @pl.when(i + 1 < n_tiles)
    def _():
        issue(i + 1, 1 - slot)

    for k in range(TOP_K):
        pltpu.make_async_copy(y_ref.at[pl.ds(0, td * SLABS)], buf.at[slot, k], sems.at[slot]).wait()

    gates = gate_ref[...]
    acc = jnp.zeros((td, D), F32)
    for k in range(TOP_K):
        yk = jnp.concatenate([buf[slot, k, pl.ds(s, td, stride=SLABS), :] for s in range(SLABS)],
                             axis=1)
        acc = acc + gates[:, k:k + 1] * yk
    o_ref[...] = h_ref[...] + g2_ref[0] * acc


def _combine(pos_flat, y_sorted, h1, gates, g2, seq):
    n_tok = h1.shape[0]
    td = TD
    n_tiles = n_tok // td
    per_b = seq // td
    return pl.pallas_call(
        functools.partial(_combine_kernel, td=td, n_tiles=n_tiles),
        out_shape=jax.ShapeDtypeStruct((n_tok, D), F32),
        grid_spec=pltpu.PrefetchScalarGridSpec(
            num_scalar_prefetch=1,
            grid=(n_tiles,),
            in_specs=[pl.BlockSpec(memory_space=pl.ANY),
                      pl.BlockSpec((td, D), lambda i, p: (i, 0)),
                      pl.BlockSpec((td, LANES), lambda i, p: (i, 0)),
                      pl.BlockSpec((1, 1, D), lambda i, p: (i // per_b, 0, 0))],
            out_specs=pl.BlockSpec((td, D), lambda i, p: (i, 0)),
            scratch_shapes=[pltpu.VMEM((2, TOP_K, td * SLABS, LANES), F32),
                            pltpu.SemaphoreType.DMA((2,))]),
        compiler_params=_params(("arbitrary",)),
    )(pos_flat, y_sorted, h1, gates, g2)


def _moe(o, w_out_bf16, h, g1, sh2, sc2, g2, w_router, b_router, w_gate, b_gate, w_up, b_up,
         w_down, b_down, layer, seq):
    n_tok = h.shape[0]
    h1, xs, idx, gates, rank, cnt = _post(o, w_out_bf16, h, g1, sh2, sc2, w_router, b_router, seq)
    counts = cnt[0, :N_EXPERTS]
    padded = (counts + BM - 1) // BM * BM
    pend = jnp.cumsum(padded)
    pstart = pend - padded
    e_idx = idx[:, :TOP_K]
    pos_flat = (pstart[e_idx] + rank[:, :TOP_K]).reshape(-1)
    n_blocks = (n_tok * TOP_K + N_EXPERTS * (BM - 1)) // BM
    block_start = jnp.arange(n_blocks, dtype=jnp.int32) * BM
    block_expert = jnp.minimum(
        jnp.sum((pend[None, :] <= block_start[:, None]).astype(jnp.int32), axis=1), N_EXPERTS - 1)
    n_valid = (pend[-1:] // BM).astype(jnp.int32)
    x_sorted = _dispatch(pos_flat, pend, counts, xs, n_blocks * BM)
    y_sorted = _experts(block_expert, n_valid, x_sorted, w_gate, b_gate, w_up, b_up, w_down, b_down,
                        layer)
    return _combine(pos_flat, y_sorted, h1, gates, g2, seq)


def _group_rms(y, ind_ref, indt_ref):
    ss = jnp.dot((y * y).astype(BF16), ind_ref[...], preferred_element_type=F32)
    r = lax.rsqrt(ss * (1.0 / HALF) + EPS)
    r_hi, r_lo = _split(r)
    d = functools.partial(jnp.dot, preferred_element_type=F32)
    return y * (d(r_hi, indt_ref[...]) + d(r_lo, indt_ref[...]))


def _qkv_kernel(h_ref, ksh_ref, ksc_ref, qsh_ref, qsc_ref, wk_ref, wv_ref, wq_ref, kn_ref, qn_ref,
                ind_ref, indt_ref, q_ref, k_ref, v_ref):
    h = h_ref[...]
    hkv = _modulate(h, ksh_ref[0], ksc_ref[0]).astype(BF16)
    xn = _modulate(h, qsh_ref[0], qsc_ref[0]).astype(BF16)
    k = jnp.dot(hkv, wk_ref[...], preferred_element_type=F32)
    v = jnp.dot(hkv, wv_ref[...], preferred_element_type=F32)
    q = jnp.dot(xn, wq_ref[...], preferred_element_type=F32)
    k_ref[...] = (_group_rms(k, ind_ref, indt_ref) * kn_ref[...]).astype(BF16)
    q_ref[...] = (_group_rms(q, ind_ref, indt_ref) * qn_ref[...] * (HALF ** -0.5)).astype(BF16)
    v_ref[...] = v.astype(BF16)


def _qkv(h, kv_shift, kv_scale, q_shift, q_scale, w_k, w_v, w_q, k_norm, q_norm, seq):
    n_tok = h.shape[0]
    per_b = seq // TM
    vec = pl.BlockSpec((1, 1, D), lambda i: (i // per_b, 0, 0))
    row = pl.BlockSpec((TM, D), lambda i: (i, 0))
    full = lambda shape: pl.BlockSpec(shape, lambda i: (0, 0))
    groups = jnp.arange(D) // HALF
    ind = (groups[:, None] == jnp.arange(LANES)[None, :]).astype(BF16)
    out = jax.ShapeDtypeStruct((n_tok, D), BF16)
    return pl.pallas_call(
        _qkv_kernel,
        out_shape=(out, out, out),
        grid=(n_tok // TM,),
        in_specs=[row, vec, vec, vec, vec, full((D, D)), full((D, D)), full((D, D)),
                  full((1, D)), full((1, D)), full((D, LANES)), full((LANES, D))],
        out_specs=(row, row, row),
        compiler_params=_params(("arbitrary",)),
    )(h, kv_shift, kv_scale, q_shift, q_scale, w_k.astype(BF16), w_v.astype(BF16), w_q.astype(BF16),
      jnp.tile(k_norm, D // HALF).reshape(1, D), jnp.tile(q_norm, D // HALF).reshape(1, D),
      ind, ind.T)


def _attn_kernel(q_ref, k_ref, v_ref, lam_ref, sub_ref, o_ref, *, tq, lambda_init):
    qi = pl.program_id(2)
    half = tq // 2
    lane = lax.broadcasted_iota(jnp.int32, (half, HEAD), 1)
    ones = jnp.ones((tq, HEAD), BF16)
    shift = CHUNK.bit_length() - 1
    diag_ok = ((lax.broadcasted_iota(jnp.int32, (half, half), 1) >> shift)
               <= (lax.broadcasted_iota(jnp.int32, (half, half), 0) >> shift))

    units = [(slab, m) for slab in range(2) for m in range(2)]
    qs = {}
    for slab, m in units:
        q = q_ref[pl.ds(slab * half, half), :]
        qs[slab, m] = jnp.where((lane < HALF) if m == 0 else (lane >= HALF), q, jnp.zeros_like(q))

    def scores(u, k):
        return lax.dot_general(qs[u], k, (((1,), (1,)), ((), ())), preferred_element_type=F32)

    def update(s, v1, carry):
        m_new = [jnp.maximum(carry[i][0], jnp.max(s[i], axis=-1, keepdims=True)) for i in range(4)]
        alpha = [jnp.exp(carry[i][0] - m_new[i]) for i in range(4)]
        p = [jnp.exp(s[i] - m_new[i]).astype(BF16) for i in range(4)]
        acc = [alpha[i] * carry[i][1] + jnp.dot(p[i], v1[i], preferred_element_type=F32) for i in range(4)]
        return tuple((m_new[i], acc[i]) for i in range(4))

    def kv_tile(kj):
        start = pl.multiple_of(kj * tq, tq)
        k = k_ref[pl.ds(start, tq), :]
        v1 = jnp.concatenate([v_ref[pl.ds(start, tq), :], ones], axis=1)
        return k, v1

    def full_tile(kj, carry):
        k, v1 = kv_tile(kj)
        return update([scores(u, k) for u in units], [v1] * 4, carry)

    def diag_tile(carry):
        k, v1 = kv_tile(qi)
        s, vs = [], []
        for u in units:
            if u[0] == 0:
                s.append(jnp.where(diag_ok, scores(u, k[:half]), NEG))
                vs.append(v1[:half])
            else:
                full = scores(u, k)
                s.append(jnp.concatenate([full[:, :half], jnp.where(diag_ok, full[:, half:], NEG)], axis=1))
                vs.append(v1)
        return update(s, vs, carry)

    init = tuple((jnp.full((half, 1), NEG, F32), jnp.zeros((half, 2 * HEAD), F32)) for _ in units)
    carry = diag_tile(lax.fori_loop(0, qi, full_tile, init))

    lp = lam_ref[...]
    lam = (jnp.exp(jnp.sum(lp[0:1] * lp[1:2], axis=-1, keepdims=True))
           - jnp.exp(jnp.sum(lp[2:3] * lp[3:4], axis=-1, keepdims=True)) + lambda_init)
    for slab in range(2):
        acc0, acc1 = carry[2 * slab][1], carry[2 * slab + 1][1]
        o = acc0[:, :HEAD] / acc0[:, HEAD:HEAD + 1] - lam * (acc1[:, :HEAD] / acc1[:, HEAD:HEAD + 1])
        o = o * lax.rsqrt(jnp.mean(o * o, axis=-1, keepdims=True) + EPS) * sub_ref[...] * (1.0 - lambda_init)
        o_ref[pl.ds(slab * half, half), :] = o.astype(o_ref.dtype)


def _attention(q, k, v, lam_params, subln, bsz, seq, lambda_init):
    tq = TQ
    nq = seq // tq
    n_tok = bsz * seq
    k3 = k.reshape(bsz, seq, D)
    v3 = v.reshape(bsz, seq, D)
    kv_spec = pl.BlockSpec((None, seq, HEAD), lambda b, h, i: (b, 0, h))
    return pl.pallas_call(
        functools.partial(_attn_kernel, tq=tq, lambda_init=lambda_init),
        out_shape=jax.ShapeDtypeStruct((n_tok, D), BF16),
        grid=(bsz, N_HEADS, nq),
        in_specs=[pl.BlockSpec((tq, HEAD), lambda b, h, i: (b * nq + i, h)), kv_spec, kv_spec,
                  pl.BlockSpec((SUBLANES, LANES), lambda b, h, i: (0, 0)),
                  pl.BlockSpec((1, HEAD), lambda b, h, i: (0, 0))],
        out_specs=pl.BlockSpec((tq, HEAD), lambda b, h, i: (b * nq + i, h)),
        compiler_params=_params(("arbitrary", "arbitrary", "arbitrary")),
    )(q, k3, v3, lam_params, subln.reshape(1, HEAD))


def kernel(x, c, ada_w, ada_b, a_w_in, a_conv, a_log, a_dt_bias, a_norm, a_w_out, kv_ada_w, kv_ada_b,
           b_w_k, b_w_v, b_k_norm, b_w_q, b_q_norm, b_lam_q1, b_lam_k1, b_lam_q2, b_lam_k2, b_subln,
           b_w_out, moe_w_router, moe_b_router, moe_w_gate, moe_b_gate, moe_w_up, moe_b_up,
           moe_w_down, moe_b_down):
    bsz, seq, _ = x.shape
    h = x.reshape(bsz * seq, D)

    def six(mod):
        return [mod[:, i * D:(i + 1) * D].reshape(bsz, 1, D) for i in range(6)]

    moe_args = (moe_w_router, moe_b_router, moe_w_gate, moe_b_gate, moe_w_up, moe_b_up,
                moe_w_down, moe_b_down)

    def moe(o, w_out, h, g1, sh2, sc2, g2, layer):
        return _moe(o, w_out.astype(BF16), h, g1, sh2, sc2, g2, moe_w_router[layer],
                    moe_b_router[layer], *moe_args[2:], layer, seq)

    sh1, sc1, g1, sh2, sc2, g2 = six(_mod_vectors(c, ada_w, ada_b, 0))
    n_in = a_w_in.shape[-1]
    w_in = jnp.pad(a_w_in[0], ((0, 0), (0, IN_COLS - n_in))).astype(BF16)
    proj = _inproj(h, sh1, sc1, w_in, seq)
    dec = jnp.zeros((SUBLANES, LANES), F32)
    dec = dec.at[0, N_HEADS:2 * N_HEADS].set(a_log[0]).at[1, N_HEADS:2 * N_HEADS].set(a_dt_bias[0])
    o = _delta(proj, a_conv[0], dec, a_norm[0].reshape(1, HEAD), bsz, seq)
    h = moe(o, a_w_out[0], h, g1, sh2, sc2, g2, 0)

    kv_mod = _mod_vectors(c, kv_ada_w, kv_ada_b, None)
    kv_shift = kv_mod[:, :D].reshape(bsz, 1, D)
    kv_scale = kv_mod[:, D:].reshape(bsz, 1, D)
    sh1, sc1, g1, sh2, sc2, g2 = six(_mod_vectors(c, ada_w, ada_b, 1))
    q, k, v = _qkv(h, kv_shift, kv_scale, sh1, sc1, b_w_k, b_w_v, b_w_q[0], b_k_norm, b_q_norm[0], seq)

    lambda_init = 0.8 - 0.6 * math.exp(-0.3 * 1)
    lam_params = jnp.zeros((SUBLANES, LANES), F32)
    for r, p in enumerate((b_lam_q1, b_lam_k1, b_lam_q2, b_lam_k2)):
        lam_params = lam_params.at[r, :HALF].set(p[0])
    o = _attention(q, k, v, lam_params, b_subln[0], bsz, seq, lambda_init)
    h = moe(o, b_w_out[0], h, g1, sh2, sc2, g2, 1)
    return h.reshape(bsz, seq, D)
```

```python
import functools
import math

import jax
import jax.numpy as jnp
from jax import lax
from jax.experimental import pallas as pl
from jax.experimental.pallas import tpu as pltpu

F32 = jnp.float32
BF16 = jnp.bfloat16

D = 1024
CHUNK = 64
N_HEADS = 8
HEAD = 128
HALF = 64
CONV_W = 4
N_EXPERTS = 32
TOP_K = 4
SWIGLU_LIMIT = 7.0
SWIGLU_ALPHA = 1.702
EPS = 1e-6
NEG = -1e30

LANES = 128
SUBLANES = 8
SLABS = D // LANES
VMEM_LIMIT = 56 * 1024 * 1024

TM = 512
T_DELTA = 256
TQ = 512
BM = 512
TD = 128
IN_COLS = 4224


def _dot(a, b):
    return jnp.dot(a.astype(BF16), b.astype(BF16), preferred_element_type=F32)


def _dot_nt(a, b):
    return lax.dot_general(a.astype(BF16), b.astype(BF16), (((1,), (1,)), ((), ())),
                           preferred_element_type=F32)


def _split(a):
    hi = a.astype(BF16)
    lo = (a - hi.astype(F32)).astype(BF16)
    return hi, lo


def _dot3(a, b):
    a_hi, a_lo = _split(a)
    b_hi, b_lo = _split(b)
    d = functools.partial(jnp.dot, preferred_element_type=F32)
    return d(a_hi, b_hi) + (d(a_hi, b_lo) + d(a_lo, b_hi))


def _sigmoid(x):
    return 1.0 / (1.0 + jnp.exp(-x))


def _silu(x):
    return x * _sigmoid(x)


def _modulate(h, shift, scale):
    ms = jnp.mean(h * h, axis=-1, keepdims=True)
    return h * lax.rsqrt(ms + EPS) * (1.0 + scale) + shift


def _params(sem):
    return pltpu.CompilerParams(dimension_semantics=sem, vmem_limit_bytes=VMEM_LIMIT)


def _mod_kernel(c_ref, w_ref, b_ref, o_ref):
    o_ref[...] = _dot3(_silu(c_ref[...]), w_ref[...]) + b_ref[...]


def _mod_vectors(c, w, b, layer):
    n = w.shape[-1]
    tn = 1024
    if layer is None:
        w_spec = pl.BlockSpec((D, tn), lambda j: (0, j))
        b2 = b.reshape(1, n)
        b_spec = pl.BlockSpec((1, tn), lambda j: (0, j))
    else:
        w_spec = pl.BlockSpec((None, D, tn), lambda j: (layer, 0, j))
        b2 = b.reshape(b.shape[0], 1, n)
        b_spec = pl.BlockSpec((None, 1, tn), lambda j: (layer, 0, j))
    bsz = c.shape[0]
    return pl.pallas_call(
        _mod_kernel,
        out_shape=jax.ShapeDtypeStruct((bsz, n), F32),
        grid=(n // tn,),
        in_specs=[pl.BlockSpec((bsz, D), lambda j: (0, 0)), w_spec, b_spec],
        out_specs=pl.BlockSpec((bsz, tn), lambda j: (0, j)),
        compiler_params=_params(("arbitrary",)),
    )(c, w, b2)


def _inproj_kernel(h_ref, sh_ref, sc_ref, w_ref, o_ref):
    xn = _modulate(h_ref[...], sh_ref[0], sc_ref[0])
    o_ref[...] = jnp.dot(xn.astype(BF16), w_ref[...], preferred_element_type=F32)


def _inproj(h, shift, scale, w_bf16, seq):
    n_tok = h.shape[0]
    per_b = seq // TM
    vec = pl.BlockSpec((1, 1, D), lambda i: (i // per_b, 0, 0))
    return pl.pallas_call(
        _inproj_kernel,
        out_shape=jax.ShapeDtypeStruct((n_tok, IN_COLS), F32),
        grid=(n_tok // TM,),
        in_specs=[pl.BlockSpec((TM, D), lambda i: (i, 0)), vec, vec,
                  pl.BlockSpec((D, IN_COLS), lambda i: (0, 0))],
        out_specs=pl.BlockSpec((TM, IN_COLS), lambda i: (i, 0)),
        compiler_params=_params(("arbitrary",)),
    )(h, shift, scale, w_bf16)


N_QKV = 3 * D


def _delta_kernel(qkv_ref, z_ref, sm_ref, cw_ref, dec_ref, nw_ref, tri_ref, o_ref, pad, state, *, t):
    @pl.when(pl.program_id(1) == 0)
    def _():
        pad[pl.ds(0, SUBLANES), :] = jnp.zeros((SUBLANES, N_QKV), F32)
        state[...] = jnp.zeros_like(state)

    pad[pl.ds(SUBLANES, t), :] = qkv_ref[...]

    sm = sm_ref[...]
    sig = _sigmoid(sm)
    dec = dec_ref[...]
    xs = sm + dec[1:2]
    softplus = jnp.maximum(xs, 0.0) + jnp.log(1.0 + jnp.exp(-jnp.abs(xs)))
    g_raw = -jnp.exp(dec[0:1]) * softplus
    g_cum = _dot3(tri_ref[...], g_raw)
    g_t = g_cum.T
    lane = lax.broadcasted_iota(jnp.int32, (t, LANES), 1)

    ri = lax.broadcasted_iota(jnp.int32, (CHUNK, CHUNK), 0)
    ci = lax.broadcasted_iota(jnp.int32, (CHUNK, CHUNK), 1)
    incl = ri >= ci
    strict = ri > ci
    eye = jnp.where(ri == ci, 1.0, 0.0)
    level_masks = []
    size = 2
    while size < CHUNK:
        shift = size.bit_length() - 1
        level_masks.append(((ri >> (shift + 1)) == (ci >> (shift + 1))) & ((ri >> shift) != (ci >> shift)))
        size *= 2
    nw = nw_ref[...]

    def conv_silu(col):
        cols = slice(col * HEAD, (col + 1) * HEAD)
        w = cw_ref[:, cols]
        y = w[3:4] * pad[pl.ds(SUBLANES, t), cols]
        for k in range(1, CONV_W):
            y = y + w[3 - k:4 - k] * pad[pl.ds(SUBLANES - k, t), cols]
        return _silu(y)

    n_chunks = t // CHUNK
    units = [(hd, c) for hd in range(N_HEADS) for c in range(n_chunks)]
    qs, ks, kbs, vbs, gcs, egs, grs = {}, {}, {}, {}, {}, {}, {}
    for hd in range(N_HEADS):
        q = conv_silu(hd)
        k = conv_silu(N_HEADS + hd)
        v = conv_silu(2 * N_HEADS + hd)
        q = q * lax.rsqrt(jnp.sum(q * q, axis=-1, keepdims=True) + EPS) * (HEAD ** -0.5)
        k = k * lax.rsqrt(jnp.sum(k * k, axis=-1, keepdims=True) + EPS)
        beta = jnp.sum(jnp.where(lane == hd, sig, 0.0), axis=-1, keepdims=True)
        g_col = jnp.sum(jnp.where(lane == hd + N_HEADS, g_cum, 0.0), axis=-1, keepdims=True)
        g_row = g_t[N_HEADS + hd:N_HEADS + hd + 1, :]
        kb = k * beta
        vb = v * beta
        eg = jnp.exp(g_col)
        for c in range(n_chunks):
            rows = slice(c * CHUNK, (c + 1) * CHUNK)
            u = (hd, c)
            qs[u], ks[u], kbs[u], vbs[u], gcs[u], egs[u] = q[rows], k[rows], kb[rows], vb[rows], g_col[rows], eg[rows]
            grs[u] = g_row[:, c * CHUNK:(c + 1) * CHUNK]

    decay = {u: jnp.where(incl, jnp.exp(jnp.where(incl, gcs[u] - grs[u], 0.0)), 0.0) for u in units}
    both = {u: _dot_nt(jnp.concatenate([kbs[u], qs[u]], axis=0), ks[u]) for u in units}
    l_mat = {u: jnp.where(strict, both[u][:CHUNK] * decay[u], 0.0) for u in units}
    qk = {u: both[u][CHUNK:] * decay[u] for u in units}
    t_inv = {u: eye - jnp.where((ri ^ ci) == 1, l_mat[u], 0.0) for u in units}
    for pair in level_masks:
        xc = {u: _dot(t_inv[u], jnp.where(pair, l_mat[u], 0.0)) for u in units}
        t_inv = {u: t_inv[u] - _dot(xc[u], t_inv[u]) for u in units}
    uw = {u: _dot(t_inv[u], jnp.concatenate([vbs[u], kbs[u] * egs[u]], axis=1)) for u in units}
    g_last = {u: gcs[u][CHUNK - 1:CHUNK] for u in units}
    k_dec_t = {u: (ks[u] * jnp.exp(g_last[u] - gcs[u])).T for u in units}
    q_dec = {u: qs[u] * egs[u] for u in units}

    heads = range(N_HEADS)
    s = [state[hd] for hd in heads]
    for c in range(n_chunks):
        ws = [_dot(jnp.concatenate([uw[hd, c][:, HEAD:], q_dec[hd, c]], axis=0), s[hd]) for hd in heads]
        v_new = [uw[hd, c][:, :HEAD] - ws[hd][:CHUNK] for hd in heads]
        o = [ws[hd][CHUNK:] + _dot(qk[hd, c], v_new[hd]) for hd in heads]
        s = [s[hd] * jnp.exp(g_last[hd, c]) + _dot(k_dec_t[hd, c], v_new[hd]) for hd in heads]
        for hd in heads:
            zc = z_ref[pl.ds(c * CHUNK, CHUNK), hd * HEAD:(hd + 1) * HEAD]
            on = o[hd] * lax.rsqrt(jnp.mean(o[hd] * o[hd], axis=-1, keepdims=True) + EPS) * nw * _silu(zc)
            o_ref[pl.ds(c * CHUNK, CHUNK), hd * HEAD:(hd + 1) * HEAD] = on.astype(o_ref.dtype)
    for hd in heads:
        state[hd] = s[hd]

    pad[pl.ds(0, SUBLANES), :] = pad[pl.ds(t, SUBLANES), :]


def _chunk_tri(t):
    r = jnp.arange(t)
    same = (r[:, None] // CHUNK) == (r[None, :] // CHUNK)
    return jnp.where(same & (r[:, None] >= r[None, :]), 1.0, 0.0).astype(F32)


def _delta(proj, conv_w, dec, norm_w, bsz, seq):
    t = T_DELTA
    nt = seq // t
    n_tok = bsz * seq
    const = lambda shape: pl.BlockSpec(shape, lambda b, j: (0, 0))
    return pl.pallas_call(
        functools.partial(_delta_kernel, t=t),
        out_shape=jax.ShapeDtypeStruct((n_tok, D), BF16),
        grid=(bsz, nt),
        in_specs=[pl.BlockSpec((t, N_QKV), lambda b, j: (b * nt + j, 0)),
                  pl.BlockSpec((t, D), lambda b, j: (b * nt + j, N_QKV // D)),
                  pl.BlockSpec((t, LANES), lambda b, j: (b * nt + j, (N_QKV + D) // LANES)),
                  const((CONV_W, N_QKV)), const((SUBLANES, LANES)), const((1, HEAD)), const((t, t))],
        out_specs=pl.BlockSpec((t, D), lambda b, j: (b * nt + j, 0)),
        scratch_shapes=[pltpu.VMEM((t + SUBLANES, N_QKV), F32),
                        pltpu.VMEM((N_HEADS, HEAD, HEAD), F32)],
        compiler_params=_params(("arbitrary", "arbitrary")),
    )(proj, proj, proj, conv_w, dec, norm_w, _chunk_tri(t))


def _post_kernel(o_ref, w_ref, h_ref, g_ref, sh_ref, sc_ref, wr_ref, br_ref, tri_ref,
                 h1_ref, xs_ref, idx_ref, gate_ref, rank_ref, cnt_ref, carry, *, tm):
    @pl.when(pl.program_id(0) == 0)
    def _():
        carry[...] = jnp.zeros_like(carry)

    y = jnp.dot(o_ref[...], w_ref[...], preferred_element_type=F32)
    h1 = h_ref[...] + g_ref[0] * y
    h1_ref[...] = h1
    xn = _modulate(h1, sh_ref[0], sc_ref[0])
    for s in range(SLABS):
        xs_ref[pl.ds(s, tm, stride=SLABS), :] = xn[:, s * LANES:(s + 1) * LANES]

    lane = lax.broadcasted_iota(jnp.int32, (tm, LANES), 1)
    lane_f = lane.astype(F32)
    logits = jnp.where(lane < N_EXPERTS, _dot3(xn, wr_ref[...]) + br_ref[...], NEG)
    vals, onehots = [], []
    for _ in range(TOP_K):
        m = jnp.max(logits, axis=-1, keepdims=True)
        first = jnp.min(jnp.where(logits == m, lane_f, float(LANES)), axis=-1, keepdims=True)
        hit = lane_f == first
        vals.append(m)
        onehots.append(hit)
        logits = jnp.where(hit, 2.0 * NEG, logits)
    exps = [jnp.exp(v - vals[0]) for v in vals]
    denom = exps[0] + exps[1] + exps[2] + exps[3]

    member = jnp.zeros((tm, LANES), F32)
    for hit in onehots:
        member = jnp.where(hit, 1.0, member)
    before = jnp.dot(tri_ref[...], member.astype(BF16), preferred_element_type=F32) + carry[...]
    idx_out = jnp.zeros((tm, LANES), F32)
    gate_out = jnp.zeros((tm, LANES), F32)
    rank_out = jnp.zeros((tm, LANES), F32)
    for k in range(TOP_K):
        first = jnp.sum(jnp.where(onehots[k], lane_f, 0.0), axis=-1, keepdims=True)
        rank = jnp.sum(jnp.where(onehots[k], before, 0.0), axis=-1, keepdims=True)
        idx_out = jnp.where(lane == k, first, idx_out)
        gate_out = jnp.where(lane == k, exps[k] / denom, gate_out)
        rank_out = jnp.where(lane == k, rank, rank_out)
    idx_ref[...] = idx_out.astype(jnp.int32)
    gate_ref[...] = gate_out
    rank_ref[...] = rank_out.astype(jnp.int32)
    carry[...] = carry[...] + jnp.sum(member, axis=0, keepdims=True)
    cnt_ref[...] = jnp.broadcast_to(carry[...], cnt_ref.shape).astype(jnp.int32)


def _post(o, w_bf16, h, gate, shift, scale, w_router, b_router, seq):
    n_tok = h.shape[0]
    tm = TM
    per_b = seq // tm
    vec = pl.BlockSpec((1, 1, D), lambda i: (i // per_b, 0, 0))
    row = pl.BlockSpec((tm, D), lambda i: (i, 0))
    small = pl.BlockSpec((tm, LANES), lambda i: (i, 0))
    r = jnp.arange(tm)
    tri = jnp.where(r[:, None] > r[None, :], 1.0, 0.0).astype(BF16)
    wr = jnp.pad(w_router, ((0, 0), (0, LANES - N_EXPERTS)))
    br = jnp.pad(b_router, (0, LANES - N_EXPERTS)).reshape(1, LANES)
    return pl.pallas_call(
        functools.partial(_post_kernel, tm=tm),
        out_shape=(jax.ShapeDtypeStruct((n_tok, D), F32),
                   jax.ShapeDtypeStruct((n_tok * SLABS, LANES), F32),
                   jax.ShapeDtypeStruct((n_tok, LANES), jnp.int32),
                   jax.ShapeDtypeStruct((n_tok, LANES), F32),
                   jax.ShapeDtypeStruct((n_tok, LANES), jnp.int32),
                   jax.ShapeDtypeStruct((SUBLANES, LANES), jnp.int32)),
        grid=(n_tok // tm,),
        in_specs=[row, pl.BlockSpec((D, D), lambda i: (0, 0)), row, vec, vec, vec,
                  pl.BlockSpec((D, LANES), lambda i: (0, 0)),
                  pl.BlockSpec((1, LANES), lambda i: (0, 0)),
                  pl.BlockSpec((tm, tm), lambda i: (0, 0))],
        out_specs=(row, pl.BlockSpec((tm * SLABS, LANES), lambda i: (i, 0)), small, small, small,
                   pl.BlockSpec((SUBLANES, LANES), lambda i: (0, 0))),
        scratch_shapes=[pltpu.VMEM((1, LANES), F32)],
        compiler_params=_params(("arbitrary",)),
    )(o, w_bf16, h, gate, shift, scale, wr, br, tri)


def _row_copy(src, src_row, dst, dst_row, sem):
    return pltpu.make_async_copy(src.at[pl.ds(pl.multiple_of(src_row * SLABS, SLABS), SLABS)],
                                 dst.at[pl.ds(pl.multiple_of(dst_row * SLABS, SLABS), SLABS)], sem)


def _dispatch_kernel(pos_ref, pend_ref, cnt_ref, x_ref, out_ref, buf, zbuf, sems, zsem, *, td, n_tiles,
                     n_blocks):
    i = pl.program_id(0)
    slot = i % 2

    def wait_slot(s):
        for _ in range(TOP_K):
            pltpu.make_async_copy(buf.at[s], out_ref.at[pl.ds(0, td * SLABS)], sems.at[s]).wait()

    @pl.when(i == 0)
    def _():
        zbuf[...] = jnp.zeros_like(zbuf)

        def zero_copy(e):
            last = pl.multiple_of((pend_ref[e] - BM) * SLABS, BM * SLABS)
            return pltpu.make_async_copy(zbuf, out_ref.at[pl.ds(last, BM * SLABS)], zsem)

        def tail_copy(j):
            start = pl.multiple_of(j * (BM * SLABS), BM * SLABS)
            return pltpu.make_async_copy(zbuf, out_ref.at[pl.ds(start, BM * SLABS)], zsem)

        def tail_start(j, c):
            tail_copy(j).start()
            return c

        def tail_wait(j, c):
            tail_copy(j).wait()
            return c

        n_valid = pend_ref[N_EXPERTS - 1] // BM
        for e in range(N_EXPERTS):
            @pl.when(cnt_ref[e] > 0)
            def _():
                zero_copy(e).start()
        lax.fori_loop(n_valid, n_blocks, tail_start, 0)
        for e in range(N_EXPERTS):
            @pl.when(cnt_ref[e] > 0)
            def _():
                zero_copy(e).wait()
        lax.fori_loop(n_valid, n_blocks, tail_wait, 0)

    @pl.when(i >= 2)
    def _():
        wait_slot(slot)

    buf[slot] = x_ref[...]

    def body(t, carry):
        for k in range(TOP_K):
            p = pos_ref[(i * td + t) * TOP_K + k]
            _row_copy(buf.at[slot], t, out_ref, p, sems.at[slot]).start()
        return carry

    lax.fori_loop(0, td, body, 0)

    @pl.when(i == n_tiles - 1)
    def _():
        wait_slot(slot)
        if n_tiles > 1:
            wait_slot(1 - slot)


def _dispatch(pos_flat, pend, counts, xs, n_rows):
    n_tok = xs.shape[0] // SLABS
    td = TD
    n_tiles = n_tok // td
    return pl.pallas_call(
        functools.partial(_dispatch_kernel, td=td, n_tiles=n_tiles, n_blocks=n_rows // BM),
        out_shape=jax.ShapeDtypeStruct((n_rows * SLABS, LANES), F32),
        grid_spec=pltpu.PrefetchScalarGridSpec(
            num_scalar_prefetch=3,
            grid=(n_tiles,),
            in_specs=[pl.BlockSpec((td * SLABS, LANES), lambda i, *_: (i, 0))],
            out_specs=pl.BlockSpec(memory_space=pl.ANY),
            scratch_shapes=[pltpu.VMEM((2, td * SLABS, LANES), F32),
                            pltpu.VMEM((BM * SLABS, LANES), F32),
                            pltpu.SemaphoreType.DMA((2,)),
                            pltpu.SemaphoreType.DMA(())]),
        compiler_params=_params(("arbitrary",)),
    )(pos_flat, pend, counts, xs)


def _expert_kernel(be_ref, nv_ref, first_ref, par_ref, nxt_ref, x_ref, wg_ref, bg_ref, wu_ref, bu_ref,
                   wd_ref, bd_ref, y_ref, wst, wg_s, wu_s, wd_s, wsem, *, layer):
    j = pl.program_id(0)

    def weight_copies(e, s):
        return [pltpu.make_async_copy(w.at[layer, e], wst.at[s, i], wsem.at[s])
                for i, w in enumerate((wg_ref, wu_ref, wd_ref))]

    @pl.when(j >= nv_ref[0])
    def _():
        y_ref[...] = jnp.zeros_like(y_ref)

    @pl.when(j < nv_ref[0])
    def _():
        @pl.when(first_ref[j] == 1)
        def _():
            s = par_ref[j]

            @pl.when(j == 0)
            def _():
                for c in weight_copies(be_ref[0], s):
                    c.start()

            for c in weight_copies(be_ref[j], s):
                c.wait()
            wg_s[...] = wst[s, 0].astype(BF16)
            wu_s[...] = wst[s, 1].astype(BF16)
            wd_s[...] = wst[s, 2].astype(BF16)

            @pl.when(nxt_ref[j] >= 0)
            def _():
                for c in weight_copies(nxt_ref[j], 1 - s):
                    c.start()

        x = jnp.concatenate([x_ref[pl.ds(s, BM, stride=SLABS), :] for s in range(SLABS)],
                            axis=1).astype(BF16)
        hg = jnp.dot(x, wg_s[...], preferred_element_type=F32) + bg_ref[...]
        hu = jnp.dot(x, wu_s[...], preferred_element_type=F32) + bu_ref[...]
        g = jnp.minimum(hg, SWIGLU_LIMIT)
        u = jnp.clip(hu, -SWIGLU_LIMIT, SWIGLU_LIMIT)
        act = g * _sigmoid(SWIGLU_ALPHA * g) * (u + 1.0)
        y = jnp.dot(act.astype(BF16), wd_s[...], preferred_element_type=F32) + bd_ref[...]
        for s in range(SLABS):
            y_ref[pl.ds(s, BM, stride=SLABS), :] = y[:, s * LANES:(s + 1) * LANES]


def _experts(block_expert, n_valid, counts, x_sorted, w_gate, b_gate, w_up, b_up, w_down, b_down, layer):
    n_blocks = block_expert.shape[0]
    blocks = jnp.arange(n_blocks, dtype=jnp.int32)
    valid = blocks < n_valid[0]
    first = valid & ((blocks == 0) | (block_expert != jnp.roll(block_expert, 1)))
    parity = (jnp.cumsum(first.astype(jnp.int32)) - 1) % 2
    experts = jnp.arange(N_EXPERTS, dtype=jnp.int32)
    later_used = (counts[None, :] > 0) & (experts[None, :] > experts[:, None])
    next_used = jnp.min(jnp.where(later_used, experts[None, :], N_EXPERTS), axis=1)
    next_expert = jnp.where(next_used < N_EXPERTS, next_used, -1)[block_expert]

    def blk(j, be, nv, *_):
        return (jnp.minimum(j, nv[0] - 1), 0)

    hbm = pl.BlockSpec(memory_space=pl.ANY)
    b_spec = pl.BlockSpec((None, None, 1, D), lambda j, be, *_: (layer, be[j], 0, 0))
    r4 = lambda b: b.reshape(b.shape[0], N_EXPERTS, 1, D)
    return pl.pallas_call(
        functools.partial(_expert_kernel, layer=layer),
        out_shape=jax.ShapeDtypeStruct(x_sorted.shape, F32),
        grid_spec=pltpu.PrefetchScalarGridSpec(
            num_scalar_prefetch=5,
            grid=(n_blocks,),
            in_specs=[pl.BlockSpec((BM * SLABS, LANES), blk), hbm, b_spec, hbm, b_spec, hbm, b_spec],
            out_specs=pl.BlockSpec((BM * SLABS, LANES), lambda j, *_: (j, 0)),
            scratch_shapes=[pltpu.VMEM((2, 3, D, D), F32),
                            pltpu.VMEM((D, D), BF16), pltpu.VMEM((D, D), BF16), pltpu.VMEM((D, D), BF16),
                            pltpu.SemaphoreType.DMA((2,))]),
        compiler_params=_params(("arbitrary",)),
    )(block_expert, n_valid, first.astype(jnp.int32), parity.astype(jnp.int32), next_expert.astype(jnp.int32),
      x_sorted, w_gate, r4(b_gate), w_up, r4(b_up), w_down, r4(b_down))


def _combine_kernel(pos_ref, y_ref, h_ref, gate_ref, g2_ref, o_ref, buf, sems, *, td, n_tiles):
    i = pl.program_id(0)
    slot = i % 2

    def issue(tile, s):
        def body(t, carry):
            for k in range(TOP_K):
                p = pos_ref[(tile * td + t) * TOP_K + k]
                _row_copy(y_ref, p, buf.at[s, k], t, sems.at[s]).start()
            return carry

        lax.fori_loop(0, td, body, 0)

    @pl.when(i == 0)
    def _():
        issue(0, 0)

    @pl.when(i + 1 < n_tiles)
    def _():
        issue(i + 1, 1 - slot)

    for k in range(TOP_K):
        pltpu.make_async_copy(y_ref.at[pl.ds(0, td * SLABS)], buf.at[slot, k], sems.at[slot]).wait()

    gates = gate_ref[...]
    acc = jnp.zeros((td, D), F32)
    for k in range(TOP_K):
        yk = jnp.concatenate([buf[slot, k, pl.ds(s, td, stride=SLABS), :] for s in range(SLABS)],
                             axis=1)
        acc = acc + gates[:, k:k + 1] * yk
    o_ref[...] = h_ref[...] + g2_ref[0] * acc


def _combine(pos_flat, y_sorted, h1, gates, g2, seq):
    n_tok = h1.shape[0]
    td = TD
    n_tiles = n_tok // td
    per_b = seq // td
    return pl.pallas_call(
        functools.partial(_combine_kernel, td=td, n_tiles=n_tiles),
        out_shape=jax.ShapeDtypeStruct((n_tok, D), F32),
        grid_spec=pltpu.PrefetchScalarGridSpec(
            num_scalar_prefetch=1,
            grid=(n_tiles,),
            in_specs=[pl.BlockSpec(memory_space=pl.ANY),
                      pl.BlockSpec((td, D), lambda i, p: (i, 0)),
                      pl.BlockSpec((td, LANES), lambda i, p: (i, 0)),
                      pl.BlockSpec((1, 1, D), lambda i, p: (i // per_b, 0, 0))],
            out_specs=pl.BlockSpec((td, D), lambda i, p: (i, 0)),
            scratch_shapes=[pltpu.VMEM((2, TOP_K, td * SLABS, LANES), F32),
                            pltpu.SemaphoreType.DMA((2,))]),
        compiler_params=_params(("arbitrary",)),
    )(pos_flat, y_sorted, h1, gates, g2)


def _moe(o, w_out_bf16, h, g1, sh2, sc2, g2, w_router, b_router, w_gate, b_gate, w_up, b_up,
         w_down, b_down, layer, seq):
    n_tok = h.shape[0]
    h1, xs, idx, gates, rank, cnt = _post(o, w_out_bf16, h, g1, sh2, sc2, w_router, b_router, seq)
    counts = cnt[0, :N_EXPERTS]
    padded = (counts + BM - 1) // BM * BM
    pend = jnp.cumsum(padded)
    pstart = pend - padded
    e_idx = idx[:, :TOP_K]
    pos_flat = (pstart[e_idx] + rank[:, :TOP_K]).reshape(-1)
    n_blocks = (n_tok * TOP_K + N_EXPERTS * (BM - 1)) // BM
    block_start = jnp.arange(n_blocks, dtype=jnp.int32) * BM
    block_expert = jnp.minimum(
        jnp.sum((pend[None, :] <= block_start[:, None]).astype(jnp.int32), axis=1), N_EXPERTS - 1)
    n_valid = (pend[-1:] // BM).astype(jnp.int32)
    x_sorted = _dispatch(pos_flat, pend, counts, xs, n_blocks * BM)
    y_sorted = _experts(block_expert, n_valid, counts, x_sorted, w_gate, b_gate, w_up, b_up, w_down,
                        b_down, layer)
    return _combine(pos_flat, y_sorted, h1, gates, g2, seq)


def _group_rms(y, ind_ref, indt_ref):
    ss = jnp.dot((y * y).astype(BF16), ind_ref[...], preferred_element_type=F32)
    r = lax.rsqrt(ss * (1.0 / HALF) + EPS)
    r_hi, r_lo = _split(r)
    d = functools.partial(jnp.dot, preferred_element_type=F32)
    return y * (d(r_hi, indt_ref[...]) + d(r_lo, indt_ref[...]))


def _qkv_kernel(h_ref, ksh_ref, ksc_ref, qsh_ref, qsc_ref, wk_ref, wv_ref, wq_ref, kn_ref, qn_ref,
                ind_ref, indt_ref, q_ref, k_ref, v_ref):
    h = h_ref[...]
    hkv = _modulate(h, ksh_ref[0], ksc_ref[0]).astype(BF16)
    xn = _modulate(h, qsh_ref[0], qsc_ref[0]).astype(BF16)
    k = jnp.dot(hkv, wk_ref[...], preferred_element_type=F32)
    v = jnp.dot(hkv, wv_ref[...], preferred_element_type=F32)
    q = jnp.dot(xn, wq_ref[...], preferred_element_type=F32)
    k_ref[...] = (_group_rms(k, ind_ref, indt_ref) * kn_ref[...]).astype(BF16)
    q_ref[...] = (_group_rms(q, ind_ref, indt_ref) * qn_ref[...] * (HALF ** -0.5)).astype(BF16)
    v_ref[...] = v.astype(BF16)


def _qkv(h, kv_shift, kv_scale, q_shift, q_scale, w_k, w_v, w_q, k_norm, q_norm, seq):
    n_tok = h.shape[0]
    per_b = seq // TM
    vec = pl.BlockSpec((1, 1, D), lambda i: (i // per_b, 0, 0))
    row = pl.BlockSpec((TM, D), lambda i: (i, 0))
    full = lambda shape: pl.BlockSpec(shape, lambda i: (0, 0))
    groups = jnp.arange(D) // HALF
    ind = (groups[:, None] == jnp.arange(LANES)[None, :]).astype(BF16)
    out = jax.ShapeDtypeStruct((n_tok, D), BF16)
    return pl.pallas_call(
        _qkv_kernel,
        out_shape=(out, out, out),
        grid=(n_tok // TM,),
        in_specs=[row, vec, vec, vec, vec, full((D, D)), full((D, D)), full((D, D)),
                  full((1, D)), full((1, D)), full((D, LANES)), full((LANES, D))],
        out_specs=(row, row, row),
        compiler_params=_params(("arbitrary",)),
    )(h, kv_shift, kv_scale, q_shift, q_scale, w_k.astype(BF16), w_v.astype(BF16), w_q.astype(BF16),
      jnp.tile(k_norm, D // HALF).reshape(1, D), jnp.tile(q_norm, D // HALF).reshape(1, D),
      ind, ind.T)


def _attn_kernel(q_ref, k_ref, v_ref, lam_ref, sub_ref, o_ref, *, tq, lambda_init):
    qi = pl.program_id(2)
    half = tq // 2
    lane = lax.broadcasted_iota(jnp.int32, (half, HEAD), 1)
    ones = jnp.ones((tq, HEAD), BF16)
    shift = CHUNK.bit_length() - 1
    diag_ok = ((lax.broadcasted_iota(jnp.int32, (half, half), 1) >> shift)
               <= (lax.broadcasted_iota(jnp.int32, (half, half), 0) >> shift))

    units = [(slab, m) for slab in range(2) for m in range(2)]
    qs = {}
    for slab, m in units:
        q = q_ref[pl.ds(slab * half, half), :]
        qs[slab, m] = jnp.where((lane < HALF) if m == 0 else (lane >= HALF), q, jnp.zeros_like(q))

    def scores(u, k):
        return lax.dot_general(qs[u], k, (((1,), (1,)), ((), ())), preferred_element_type=F32)

    def update(s, v1, carry):
        m_new = [jnp.maximum(carry[i][0], jnp.max(s[i], axis=-1, keepdims=True)) for i in range(4)]
        alpha = [jnp.exp(carry[i][0] - m_new[i]) for i in range(4)]
        p = [jnp.exp(s[i] - m_new[i]).astype(BF16) for i in range(4)]
        acc = [alpha[i] * carry[i][1] + jnp.dot(p[i], v1[i], preferred_element_type=F32) for i in range(4)]
        return tuple((m_new[i], acc[i]) for i in range(4))

    def kv_tile(kj):
        start = pl.multiple_of(kj * tq, tq)
        k = k_ref[pl.ds(start, tq), :]
        v1 = jnp.concatenate([v_ref[pl.ds(start, tq), :], ones], axis=1)
        return k, v1

    def full_tile(kj, carry):
        k, v1 = kv_tile(kj)
        return update([scores(u, k) for u in units], [v1] * 4, carry)

    def diag_tile(carry):
        k, v1 = kv_tile(qi)
        s, vs = [], []
        for u in units:
            if u[0] == 0:
                s.append(jnp.where(diag_ok, scores(u, k[:half]), NEG))
                vs.append(v1[:half])
            else:
                full = scores(u, k)
                s.append(jnp.concatenate([full[:, :half], jnp.where(diag_ok, full[:, half:], NEG)], axis=1))
                vs.append(v1)
        return update(s, vs, carry)

    init = tuple((jnp.full((half, 1), NEG, F32), jnp.zeros((half, 2 * HEAD), F32)) for _ in units)
    carry = diag_tile(lax.fori_loop(0, qi, full_tile, init))

    lp = lam_ref[...]
    lam = (jnp.exp(jnp.sum(lp[0:1] * lp[1:2], axis=-1, keepdims=True))
           - jnp.exp(jnp.sum(lp[2:3] * lp[3:4], axis=-1, keepdims=True)) + lambda_init)
    for slab in range(2):
        acc0, acc1 = carry[2 * slab][1], carry[2 * slab + 1][1]
        o = acc0[:, :HEAD] / acc0[:, HEAD:HEAD + 1] - lam * (acc1[:, :HEAD] / acc1[:, HEAD:HEAD + 1])
        o = o * lax.rsqrt(jnp.mean(o * o, axis=-1, keepdims=True) + EPS) * sub_ref[...] * (1.0 - lambda_init)
        o_ref[pl.ds(slab * half, half), :] = o.astype(o_ref.dtype)


def _attention(q, k, v, lam_params, subln, bsz, seq, lambda_init):
    tq = TQ
    nq = seq // tq
    n_tok = bsz * seq
    k3 = k.reshape(bsz, seq, D)
    v3 = v.reshape(bsz, seq, D)
    kv_spec = pl.BlockSpec((None, seq, HEAD), lambda b, h, i: (b, 0, h))
    return pl.pallas_call(
        functools.partial(_attn_kernel, tq=tq, lambda_init=lambda_init),
        out_shape=jax.ShapeDtypeStruct((n_tok, D), BF16),
        grid=(bsz, N_HEADS, nq),
        in_specs=[pl.BlockSpec((tq, HEAD), lambda b, h, i: (b * nq + i, h)), kv_spec, kv_spec,
                  pl.BlockSpec((SUBLANES, LANES), lambda b, h, i: (0, 0)),
                  pl.BlockSpec((1, HEAD), lambda b, h, i: (0, 0))],
        out_specs=pl.BlockSpec((tq, HEAD), lambda b, h, i: (b * nq + i, h)),
        compiler_params=_params(("arbitrary", "arbitrary", "arbitrary")),
    )(q, k3, v3, lam_params, subln.reshape(1, HEAD))


def kernel(x, c, ada_w, ada_b, a_w_in, a_conv, a_log, a_dt_bias, a_norm, a_w_out, kv_ada_w, kv_ada_b,
           b_w_k, b_w_v, b_k_norm, b_w_q, b_q_norm, b_lam_q1, b_lam_k1, b_lam_q2, b_lam_k2, b_subln,
           b_w_out, moe_w_router, moe_b_router, moe_w_gate, moe_b_gate, moe_w_up, moe_b_up,
           moe_w_down, moe_b_down):
    bsz, seq, _ = x.shape
    h = x.reshape(bsz * seq, D)

    def six(mod):
        return [mod[:, i * D:(i + 1) * D].reshape(bsz, 1, D) for i in range(6)]

    moe_args = (moe_w_router, moe_b_router, moe_w_gate, moe_b_gate, moe_w_up, moe_b_up,
                moe_w_down, moe_b_down)

    def moe(o, w_out, h, g1, sh2, sc2, g2, layer):
        return _moe(o, w_out.astype(BF16), h, g1, sh2, sc2, g2, moe_w_router[layer],
                    moe_b_router[layer], *moe_args[2:], layer, seq)

    sh1, sc1, g1, sh2, sc2, g2 = six(_mod_vectors(c, ada_w, ada_b, 0))
    n_in = a_w_in.shape[-1]
    w_in = jnp.pad(a_w_in[0], ((0, 0), (0, IN_COLS - n_in))).astype(BF16)
    proj = _inproj(h, sh1, sc1, w_in, seq)
    dec = jnp.zeros((SUBLANES, LANES), F32)
    dec = dec.at[0, N_HEADS:2 * N_HEADS].set(a_log[0]).at[1, N_HEADS:2 * N_HEADS].set(a_dt_bias[0])
    o = _delta(proj, a_conv[0], dec, a_norm[0].reshape(1, HEAD), bsz, seq)
    h = moe(o, a_w_out[0], h, g1, sh2, sc2, g2, 0)

    kv_mod = _mod_vectors(c, kv_ada_w, kv_ada_b, None)
    kv_shift = kv_mod[:, :D].reshape(bsz, 1, D)
    kv_scale = kv_mod[:, D:].reshape(bsz, 1, D)
    sh1, sc1, g1, sh2, sc2, g2 = six(_mod_vectors(c, ada_w, ada_b, 1))
    q, k, v = _qkv(h, kv_shift, kv_scale, sh1, sc1, b_w_k, b_w_v, b_w_q[0], b_k_norm, b_q_norm[0], seq)

    lambda_init = 0.8 - 0.6 * math.exp(-0.3 * 1)
    lam_params = jnp.zeros((SUBLANES, LANES), F32)
    for r, p in enumerate((b_lam_q1, b_lam_k1, b_lam_q2, b_lam_k2)):
        lam_params = lam_params.at[r, :HALF].set(p[0])
    o = _attention(q, k, v, lam_params, b_subln[0], bsz, seq, lambda_init)
    h = moe(o, b_w_out[0], h, g1, sh2, sc2, g2, 1)
    return h.reshape(bsz, seq, D)
```

```python
import functools
import math

import jax
import jax.numpy as jnp
from jax import lax
from jax.experimental import pallas as pl
from jax.experimental.pallas import tpu as pltpu

F32 = jnp.float32
BF16 = jnp.bfloat16

D = 1024
CHUNK = 64
N_HEADS = 8
HEAD = 128
HALF = 64
CONV_W = 4
N_EXPERTS = 32
TOP_K = 4
SWIGLU_LIMIT = 7.0
SWIGLU_ALPHA = 1.702
EPS = 1e-6
NEG = -1e30

LANES = 128
SUBLANES = 8
SLABS = D // LANES
VMEM_LIMIT = 56 * 1024 * 1024

TM = 512
T_DELTA = 256
TQ = 512
BM = 512
TD = 128
IN_COLS = 4224


def _dot(a, b):
    return jnp.dot(a.astype(BF16), b.astype(BF16), preferred_element_type=F32)


def _dot_nt(a, b):
    return lax.dot_general(a.astype(BF16), b.astype(BF16), (((1,), (1,)), ((), ())),
                           preferred_element_type=F32)


def _split(a):
    hi = a.astype(BF16)
    lo = (a - hi.astype(F32)).astype(BF16)
    return hi, lo


def _dot3(a, b):
    a_hi, a_lo = _split(a)
    b_hi, b_lo = _split(b)
    d = functools.partial(jnp.dot, preferred_element_type=F32)
    return d(a_hi, b_hi) + (d(a_hi, b_lo) + d(a_lo, b_hi))


def _sigmoid(x):
    return 1.0 / (1.0 + jnp.exp(-x))


def _silu(x):
    return x * _sigmoid(x)


def _modulate(h, shift, scale):
    ms = jnp.mean(h * h, axis=-1, keepdims=True)
    return h * lax.rsqrt(ms + EPS) * (1.0 + scale) + shift


def _params(sem):
    return pltpu.CompilerParams(dimension_semantics=sem, vmem_limit_bytes=VMEM_LIMIT)


def _mod_kernel(c_ref, w_ref, b_ref, o_ref):
    o_ref[...] = _dot3(_silu(c_ref[...]), w_ref[...]) + b_ref[...]


def _mod_vectors(c, w, b, layer):
    n = w.shape[-1]
    tn = 1024
    if layer is None:
        w_spec = pl.BlockSpec((D, tn), lambda j: (0, j))
        b2 = b.reshape(1, n)
        b_spec = pl.BlockSpec((1, tn), lambda j: (0, j))
    else:
        w_spec = pl.BlockSpec((None, D, tn), lambda j: (layer, 0, j))
        b2 = b.reshape(b.shape[0], 1, n)
        b_spec = pl.BlockSpec((None, 1, tn), lambda j: (layer, 0, j))
    bsz = c.shape[0]
    return pl.pallas_call(
        _mod_kernel,
        out_shape=jax.ShapeDtypeStruct((bsz, n), F32),
        grid=(n // tn,),
        in_specs=[pl.BlockSpec((bsz, D), lambda j: (0, 0)), w_spec, b_spec],
        out_specs=pl.BlockSpec((bsz, tn), lambda j: (0, j)),
        compiler_params=_params(("arbitrary",)),
    )(c, w, b2)


def _inproj_kernel(h_ref, sh_ref, sc_ref, w_ref, o_ref):
    xn = _modulate(h_ref[...], sh_ref[0], sc_ref[0])
    o_ref[...] = jnp.dot(xn.astype(BF16), w_ref[...], preferred_element_type=F32)


def _inproj(h, shift, scale, w_bf16, seq):
    n_tok = h.shape[0]
    per_b = seq // TM
    vec = pl.BlockSpec((1, 1, D), lambda i: (i // per_b, 0, 0))
    return pl.pallas_call(
        _inproj_kernel,
        out_shape=jax.ShapeDtypeStruct((n_tok, IN_COLS), F32),
        grid=(n_tok // TM,),
        in_specs=[pl.BlockSpec((TM, D), lambda i: (i, 0)), vec, vec,
                  pl.BlockSpec((D, IN_COLS), lambda i: (0, 0))],
        out_specs=pl.BlockSpec((TM, IN_COLS), lambda i: (i, 0)),
        compiler_params=_params(("arbitrary",)),
    )(h, shift, scale, w_bf16)


N_QKV = 3 * D


def _delta_kernel(qkv_ref, z_ref, sm_ref, cw_ref, dec_ref, nw_ref, tri_ref, o_ref, pad, state, *, t):
    @pl.when(pl.program_id(1) == 0)
    def _():
        pad[pl.ds(0, SUBLANES), :] = jnp.zeros((SUBLANES, N_QKV), F32)
        state[...] = jnp.zeros_like(state)

    pad[pl.ds(SUBLANES, t), :] = qkv_ref[...]

    sm = sm_ref[...]
    sig = _sigmoid(sm)
    dec = dec_ref[...]
    xs = sm + dec[1:2]
    softplus = jnp.maximum(xs, 0.0) + jnp.log(1.0 + jnp.exp(-jnp.abs(xs)))
    g_raw = -jnp.exp(dec[0:1]) * softplus
    g_cum = _dot3(tri_ref[...], g_raw)
    g_t = g_cum.T
    lane = lax.broadcasted_iota(jnp.int32, (t, LANES), 1)

    ri = lax.broadcasted_iota(jnp.int32, (CHUNK, CHUNK), 0)
    ci = lax.broadcasted_iota(jnp.int32, (CHUNK, CHUNK), 1)
    incl = ri >= ci
    strict = ri > ci
    eye = jnp.where(ri == ci, 1.0, 0.0)
    level_masks = []
    size = 2
    while size < CHUNK:
        shift = size.bit_length() - 1
        level_masks.append(((ri >> (shift + 1)) == (ci >> (shift + 1))) & ((ri >> shift) != (ci >> shift)))
        size *= 2
    nw = nw_ref[...]

    def conv_silu(col):
        cols = slice(col * HEAD, (col + 1) * HEAD)
        w = cw_ref[:, cols]
        y = w[3:4] * pad[pl.ds(SUBLANES, t), cols]
        for k in range(1, CONV_W):
            y = y + w[3 - k:4 - k] * pad[pl.ds(SUBLANES - k, t), cols]
        return _silu(y)

    n_chunks = t // CHUNK
    units = [(hd, c) for hd in range(N_HEADS) for c in range(n_chunks)]
    qs, ks, kbs, vbs, gcs, egs, grs = {}, {}, {}, {}, {}, {}, {}
    for hd in range(N_HEADS):
        q = conv_silu(hd)
        k = conv_silu(N_HEADS + hd)
        v = conv_silu(2 * N_HEADS + hd)
        q = q * lax.rsqrt(jnp.sum(q * q, axis=-1, keepdims=True) + EPS) * (HEAD ** -0.5)
        k = k * lax.rsqrt(jnp.sum(k * k, axis=-1, keepdims=True) + EPS)
        beta = jnp.sum(jnp.where(lane == hd, sig, 0.0), axis=-1, keepdims=True)
        g_col = jnp.sum(jnp.where(lane == hd + N_HEADS, g_cum, 0.0), axis=-1, keepdims=True)
        g_row = g_t[N_HEADS + hd:N_HEADS + hd + 1, :]
        kb = k * beta
        vb = v * beta
        eg = jnp.exp(g_col)
        for c in range(n_chunks):
            rows = slice(c * CHUNK, (c + 1) * CHUNK)
            u = (hd, c)
            qs[u], ks[u], kbs[u], vbs[u], gcs[u], egs[u] = q[rows], k[rows], kb[rows], vb[rows], g_col[rows], eg[rows]
            grs[u] = g_row[:, c * CHUNK:(c + 1) * CHUNK]

    decay = {u: jnp.where(incl, jnp.exp(jnp.where(incl, gcs[u] - grs[u], 0.0)), 0.0) for u in units}
    both = {u: _dot_nt(jnp.concatenate([kbs[u], qs[u]], axis=0), ks[u]) for u in units}
    l_mat = {u: jnp.where(strict, both[u][:CHUNK] * decay[u], 0.0) for u in units}
    qk = {u: both[u][CHUNK:] * decay[u] for u in units}
    t_inv = {u: eye - jnp.where((ri ^ ci) == 1, l_mat[u], 0.0) for u in units}
    for pair in level_masks:
        xc = {u: _dot(t_inv[u], jnp.where(pair, l_mat[u], 0.0)) for u in units}
        t_inv = {u: t_inv[u] - _dot(xc[u], t_inv[u]) for u in units}
    uw = {u: _dot(t_inv[u], jnp.concatenate([vbs[u], kbs[u] * egs[u]], axis=1)) for u in units}
    g_last = {u: gcs[u][CHUNK - 1:CHUNK] for u in units}
    k_dec_t = {u: (ks[u] * jnp.exp(g_last[u] - gcs[u])).T for u in units}
    q_dec = {u: qs[u] * egs[u] for u in units}

    heads = range(N_HEADS)
    s = [state[hd] for hd in heads]
    for c in range(n_chunks):
        ws = [_dot(jnp.concatenate([uw[hd, c][:, HEAD:], q_dec[hd, c]], axis=0), s[hd]) for hd in heads]
        v_new = [uw[hd, c][:, :HEAD] - ws[hd][:CHUNK] for hd in heads]
        o = [ws[hd][CHUNK:] + _dot(qk[hd, c], v_new[hd]) for hd in heads]
        s = [s[hd] * jnp.exp(g_last[hd, c]) + _dot(k_dec_t[hd, c], v_new[hd]) for hd in heads]
        for hd in heads:
            zc = z_ref[pl.ds(c * CHUNK, CHUNK), hd * HEAD:(hd + 1) * HEAD]
            on = o[hd] * lax.rsqrt(jnp.mean(o[hd] * o[hd], axis=-1, keepdims=True) + EPS) * nw * _silu(zc)
            o_ref[pl.ds(c * CHUNK, CHUNK), hd * HEAD:(hd + 1) * HEAD] = on.astype(o_ref.dtype)
    for hd in heads:
        state[hd] = s[hd]

    pad[pl.ds(0, SUBLANES), :] = pad[pl.ds(t, SUBLANES), :]


def _chunk_tri(t):
    r = jnp.arange(t)
    same = (r[:, None] // CHUNK) == (r[None, :] // CHUNK)
    return jnp.where(same & (r[:, None] >= r[None, :]), 1.0, 0.0).astype(F32)


def _delta(proj, conv_w, dec, norm_w, bsz, seq):
    t = T_DELTA
    nt = seq // t
    n_tok = bsz * seq
    const = lambda shape: pl.BlockSpec(shape, lambda b, j: (0, 0))
    return pl.pallas_call(
        functools.partial(_delta_kernel, t=t),
        out_shape=jax.ShapeDtypeStruct((n_tok, D), BF16),
        grid=(bsz, nt),
        in_specs=[pl.BlockSpec((t, N_QKV), lambda b, j: (b * nt + j, 0)),
                  pl.BlockSpec((t, D), lambda b, j: (b * nt + j, N_QKV // D)),
                  pl.BlockSpec((t, LANES), lambda b, j: (b * nt + j, (N_QKV + D) // LANES)),
                  const((CONV_W, N_QKV)), const((SUBLANES, LANES)), const((1, HEAD)), const((t, t))],
        out_specs=pl.BlockSpec((t, D), lambda b, j: (b * nt + j, 0)),
        scratch_shapes=[pltpu.VMEM((t + SUBLANES, N_QKV), F32),
                        pltpu.VMEM((N_HEADS, HEAD, HEAD), F32)],
        compiler_params=_params(("arbitrary", "arbitrary")),
    )(proj, proj, proj, conv_w, dec, norm_w, _chunk_tri(t))


def _post_kernel(o_ref, w_ref, h_ref, g_ref, sh_ref, sc_ref, wr_ref, br_ref, tri_ref,
                 h1_ref, xs_ref, idx_ref, gate_ref, rank_ref, cnt_ref, carry, *, tm):
    @pl.when(pl.program_id(0) == 0)
    def _():
        carry[...] = jnp.zeros_like(carry)

    y = jnp.dot(o_ref[...], w_ref[...], preferred_element_type=F32)
    h1 = h_ref[...] + g_ref[0] * y
    h1_ref[...] = h1
    xn = _modulate(h1, sh_ref[0], sc_ref[0])
    for s in range(SLABS):
        xs_ref[pl.ds(s, tm, stride=SLABS), :] = xn[:, s * LANES:(s + 1) * LANES]

    lane = lax.broadcasted_iota(jnp.int32, (tm, LANES), 1)
    lane_f = lane.astype(F32)
    logits = jnp.where(lane < N_EXPERTS, _dot3(xn, wr_ref[...]) + br_ref[...], NEG)
    vals, onehots = [], []
    for _ in range(TOP_K):
        m = jnp.max(logits, axis=-1, keepdims=True)
        first = jnp.min(jnp.where(logits == m, lane_f, float(LANES)), axis=-1, keepdims=True)
        hit = lane_f == first
        vals.append(m)
        onehots.append(hit)
        logits = jnp.where(hit, 2.0 * NEG, logits)
    exps = [jnp.exp(v - vals[0]) for v in vals]
    denom = exps[0] + exps[1] + exps[2] + exps[3]

    member = jnp.zeros((tm, LANES), F32)
    for hit in onehots:
        member = jnp.where(hit, 1.0, member)
    before = jnp.dot(tri_ref[...], member.astype(BF16), preferred_element_type=F32) + carry[...]
    idx_out = jnp.zeros((tm, LANES), F32)
    gate_out = jnp.zeros((tm, LANES), F32)
    rank_out = jnp.zeros((tm, LANES), F32)
    for k in range(TOP_K):
        first = jnp.sum(jnp.where(onehots[k], lane_f, 0.0), axis=-1, keepdims=True)
        rank = jnp.sum(jnp.where(onehots[k], before, 0.0), axis=-1, keepdims=True)
        idx_out = jnp.where(lane == k, first, idx_out)
        gate_out = jnp.where(lane == k, exps[k] / denom, gate_out)
        rank_out = jnp.where(lane == k, rank, rank_out)
    idx_ref[...] = idx_out.astype(jnp.int32)
    gate_ref[...] = gate_out
    rank_ref[...] = rank_out.astype(jnp.int32)
    carry[...] = carry[...] + jnp.sum(member, axis=0, keepdims=True)
    cnt_ref[...] = jnp.broadcast_to(carry[...], cnt_ref.shape).astype(jnp.int32)


def _post(o, w_bf16, h, gate, shift, scale, w_router, b_router, seq):
    n_tok = h.shape[0]
    tm = TM
    per_b = seq // tm
    vec = pl.BlockSpec((1, 1, D), lambda i: (i // per_b, 0, 0))
    row = pl.BlockSpec((tm, D), lambda i: (i, 0))
    small = pl.BlockSpec((tm, LANES), lambda i: (i, 0))
    r = jnp.arange(tm)
    tri = jnp.where(r[:, None] > r[None, :], 1.0, 0.0).astype(BF16)
    wr = jnp.pad(w_router, ((0, 0), (0, LANES - N_EXPERTS)))
    br = jnp.pad(b_router, (0, LANES - N_EXPERTS)).reshape(1, LANES)
    return pl.pallas_call(
        functools.partial(_post_kernel, tm=tm),
        out_shape=(jax.ShapeDtypeStruct((n_tok, D), F32),
                   jax.ShapeDtypeStruct((n_tok * SLABS, LANES), F32),
                   jax.ShapeDtypeStruct((n_tok, LANES), jnp.int32),
                   jax.ShapeDtypeStruct((n_tok, LANES), F32),
                   jax.ShapeDtypeStruct((n_tok, LANES), jnp.int32),
                   jax.ShapeDtypeStruct((SUBLANES, LANES), jnp.int32)),
        grid=(n_tok // tm,),
        in_specs=[row, pl.BlockSpec((D, D), lambda i: (0, 0)), row, vec, vec, vec,
                  pl.BlockSpec((D, LANES), lambda i: (0, 0)),
                  pl.BlockSpec((1, LANES), lambda i: (0, 0)),
                  pl.BlockSpec((tm, tm), lambda i: (0, 0))],
        out_specs=(row, pl.BlockSpec((tm * SLABS, LANES), lambda i: (i, 0)), small, small, small,
                   pl.BlockSpec((SUBLANES, LANES), lambda i: (0, 0))),
        scratch_shapes=[pltpu.VMEM((1, LANES), F32)],
        compiler_params=_params(("arbitrary",)),
    )(o, w_bf16, h, gate, shift, scale, wr, br, tri)


def _row_copy(src, src_row, dst, dst_row, sem):
    return pltpu.make_async_copy(src.at[pl.ds(pl.multiple_of(src_row * SLABS, SLABS), SLABS)],
                                 dst.at[pl.ds(pl.multiple_of(dst_row * SLABS, SLABS), SLABS)], sem)


def _dispatch_kernel(e_ref, r_ref, ps_ref, pend_ref, cnt_ref, x_ref, out_ref, buf, zbuf, sems, zsem, *, td, n_tiles,
                     n_blocks):
    i = pl.program_id(0)
    slot = i % 2

    def wait_slot(s):
        for _ in range(TOP_K):
            pltpu.make_async_copy(buf.at[s], out_ref.at[pl.ds(0, td * SLABS)], sems.at[s]).wait()

    @pl.when(i == 0)
    def _():
        zbuf[...] = jnp.zeros_like(zbuf)

        def zero_copy(e):
            last = pl.multiple_of((pend_ref[e] - BM) * SLABS, BM * SLABS)
            return pltpu.make_async_copy(zbuf, out_ref.at[pl.ds(last, BM * SLABS)], zsem)

        def tail_copy(j):
            start = pl.multiple_of(j * (BM * SLABS), BM * SLABS)
            return pltpu.make_async_copy(zbuf, out_ref.at[pl.ds(start, BM * SLABS)], zsem)

        def tail_start(j, c):
            tail_copy(j).start()
            return c

        def tail_wait(j, c):
            tail_copy(j).wait()
            return c

        n_valid = pend_ref[N_EXPERTS - 1] // BM
        for e in range(N_EXPERTS):
            @pl.when(cnt_ref[e] > 0)
            def _():
                zero_copy(e).start()
        lax.fori_loop(n_valid, n_blocks, tail_start, 0)
        for e in range(N_EXPERTS):
            @pl.when(cnt_ref[e] > 0)
            def _():
                zero_copy(e).wait()
        lax.fori_loop(n_valid, n_blocks, tail_wait, 0)

    @pl.when(i >= 2)
    def _():
        wait_slot(slot)

    buf[slot] = x_ref[...]

    def body(t, carry):
        for k in range(TOP_K):
            pair = (i * td + t) * TOP_K + k
            p = ps_ref[e_ref[pair]] + r_ref[pair]
            _row_copy(buf.at[slot], t, out_ref, p, sems.at[slot]).start()
        return carry

    lax.fori_loop(0, td, body, 0)

    @pl.when(i == n_tiles - 1)
    def _():
        wait_slot(slot)
        if n_tiles > 1:
            wait_slot(1 - slot)


def _dispatch(e_flat, rank_flat, pstart, pend, counts, xs, n_rows):
    n_tok = xs.shape[0] // SLABS
    td = TD
    n_tiles = n_tok // td
    return pl.pallas_call(
        functools.partial(_dispatch_kernel, td=td, n_tiles=n_tiles, n_blocks=n_rows // BM),
        out_shape=jax.ShapeDtypeStruct((n_rows * SLABS, LANES), F32),
        grid_spec=pltpu.PrefetchScalarGridSpec(
            num_scalar_prefetch=5,
            grid=(n_tiles,),
            in_specs=[pl.BlockSpec((td * SLABS, LANES), lambda i, *_: (i, 0))],
            out_specs=pl.BlockSpec(memory_space=pl.ANY),
            scratch_shapes=[pltpu.VMEM((2, td * SLABS, LANES), F32),
                            pltpu.VMEM((BM * SLABS, LANES), F32),
                            pltpu.SemaphoreType.DMA((2,)),
                            pltpu.SemaphoreType.DMA(())]),
        compiler_params=_params(("arbitrary",)),
    )(e_flat, rank_flat, pstart, pend, counts, xs)


def _expert_kernel(be_ref, nv_ref, first_ref, par_ref, nxt_ref, x_ref, wg_ref, bg_ref, wu_ref, bu_ref,
                   wd_ref, bd_ref, y_ref, wst, wg_s, wu_s, wd_s, wsem, *, layer):
    j = pl.program_id(0)

    def weight_copies(e, s):
        return [pltpu.make_async_copy(w.at[layer, e], wst.at[s, i], wsem.at[s])
                for i, w in enumerate((wg_ref, wu_ref, wd_ref))]

    @pl.when(j >= nv_ref[0])
    def _():
        y_ref[...] = jnp.zeros_like(y_ref)

    @pl.when(j < nv_ref[0])
    def _():
        @pl.when(first_ref[j] == 1)
        def _():
            s = par_ref[j]

            @pl.when(j == 0)
            def _():
                for c in weight_copies(be_ref[0], s):
                    c.start()

            for c in weight_copies(be_ref[j], s):
                c.wait()
            wg_s[...] = wst[s, 0].astype(BF16)
            wu_s[...] = wst[s, 1].astype(BF16)
            wd_s[...] = wst[s, 2].astype(BF16)

            @pl.when(nxt_ref[j] >= 0)
            def _():
                for c in weight_copies(nxt_ref[j], 1 - s):
                    c.start()

        x = jnp.concatenate([x_ref[pl.ds(s, BM, stride=SLABS), :] for s in range(SLABS)],
                            axis=1).astype(BF16)
        hg = jnp.dot(x, wg_s[...], preferred_element_type=F32) + bg_ref[...]
        hu = jnp.dot(x, wu_s[...], preferred_element_type=F32) + bu_ref[...]
        g = jnp.minimum(hg, SWIGLU_LIMIT)
        u = jnp.clip(hu, -SWIGLU_LIMIT, SWIGLU_LIMIT)
        act = g * _sigmoid(SWIGLU_ALPHA * g) * (u + 1.0)
        y = jnp.dot(act.astype(BF16), wd_s[...], preferred_element_type=F32) + bd_ref[...]
        for s in range(SLABS):
            y_ref[pl.ds(s, BM, stride=SLABS), :] = y[:, s * LANES:(s + 1) * LANES]


def _experts(block_expert, n_valid, counts, x_sorted, w_gate, b_gate, w_up, b_up, w_down, b_down, layer):
    n_blocks = block_expert.shape[0]
    blocks = jnp.arange(n_blocks, dtype=jnp.int32)
    valid = blocks < n_valid[0]
    first = valid & ((blocks == 0) | (block_expert != jnp.roll(block_expert, 1)))
    parity = (jnp.cumsum(first.astype(jnp.int32)) - 1) % 2
    experts = jnp.arange(N_EXPERTS, dtype=jnp.int32)
    later_used = (counts[None, :] > 0) & (experts[None, :] > experts[:, None])
    next_used = jnp.min(jnp.where(later_used, experts[None, :], N_EXPERTS), axis=1)
    next_expert = jnp.where(next_used < N_EXPERTS, next_used, -1)[block_expert]

    def blk(j, be, nv, *_):
        return (jnp.minimum(j, nv[0] - 1), 0)

    hbm = pl.BlockSpec(memory_space=pl.ANY)
    b_spec = pl.BlockSpec((None, None, 1, D), lambda j, be, *_: (layer, be[j], 0, 0))
    r4 = lambda b: b.reshape(b.shape[0], N_EXPERTS, 1, D)
    return pl.pallas_call(
        functools.partial(_expert_kernel, layer=layer),
        out_shape=jax.ShapeDtypeStruct(x_sorted.shape, F32),
        grid_spec=pltpu.PrefetchScalarGridSpec(
            num_scalar_prefetch=5,
            grid=(n_blocks,),
            in_specs=[pl.BlockSpec((BM * SLABS, LANES), blk), hbm, b_spec, hbm, b_spec, hbm, b_spec],
            out_specs=pl.BlockSpec((BM * SLABS, LANES), lambda j, *_: (j, 0)),
            scratch_shapes=[pltpu.VMEM((2, 3, D, D), F32),
                            pltpu.VMEM((D, D), BF16), pltpu.VMEM((D, D), BF16), pltpu.VMEM((D, D), BF16),
                            pltpu.SemaphoreType.DMA((2,))]),
        compiler_params=_params(("arbitrary",)),
    )(block_expert, n_valid, first.astype(jnp.int32), parity.astype(jnp.int32), next_expert.astype(jnp.int32),
      x_sorted, w_gate, r4(b_gate), w_up, r4(b_up), w_down, r4(b_down))


def _combine_kernel(e_ref, r_ref, ps_ref, y_ref, h_ref, gate_ref, g2_ref, o_ref, buf, sems, *, td, n_tiles):
    i = pl.program_id(0)
    slot = i % 2

    def issue(tile, s):
        def body(t, carry):
            for k in range(TOP_K):
                pair = (tile * td + t) * TOP_K + k
                p = ps_ref[e_ref[pair]] + r_ref[pair]
                _row_copy(y_ref, p, buf.at[s, k], t, sems.at[s]).start()
            return carry

        lax.fori_loop(0, td, body, 0)

    @pl.when(i == 0)
    def _():
        issue(0, 0)

    @pl.when(i + 1 < n_tiles)
    def _():
        issue(i + 1, 1 - slot)

    for k in range(TOP_K):
        pltpu.make_async_copy(y_ref.at[pl.ds(0, td * SLABS)], buf.at[slot, k], sems.at[slot]).wait()

    gates = gate_ref[...]
    acc = jnp.zeros((td, D), F32)
    for k in range(TOP_K):
        yk = jnp.concatenate([buf[slot, k, pl.ds(s, td, stride=SLABS), :] for s in range(SLABS)],
                             axis=1)
        acc = acc + gates[:, k:k + 1] * yk
    o_ref[...] = h_ref[...] + g2_ref[0] * acc


def _combine(e_flat, rank_flat, pstart, y_sorted, h1, gates, g2, seq):
    n_tok = h1.shape[0]
    td = TD
    n_tiles = n_tok // td
    per_b = seq // td
    return pl.pallas_call(
        functools.partial(_combine_kernel, td=td, n_tiles=n_tiles),
        out_shape=jax.ShapeDtypeStruct((n_tok, D), F32),
        grid_spec=pltpu.PrefetchScalarGridSpec(
            num_scalar_prefetch=3,
            grid=(n_tiles,),
            in_specs=[pl.BlockSpec(memory_space=pl.ANY),
                      pl.BlockSpec((td, D), lambda i, *_: (i, 0)),
                      pl.BlockSpec((td, LANES), lambda i, *_: (i, 0)),
                      pl.BlockSpec((1, 1, D), lambda i, *_: (i // per_b, 0, 0))],
            out_specs=pl.BlockSpec((td, D), lambda i, *_: (i, 0)),
            scratch_shapes=[pltpu.VMEM((2, TOP_K, td * SLABS, LANES), F32),
                            pltpu.SemaphoreType.DMA((2,))]),
        compiler_params=_params(("arbitrary",)),
    )(e_flat, rank_flat, pstart, y_sorted, h1, gates, g2)


def _moe(o, w_out_bf16, h, g1, sh2, sc2, g2, w_router, b_router, w_gate, b_gate, w_up, b_up,
         w_down, b_down, layer, seq):
    n_tok = h.shape[0]
    h1, xs, idx, gates, rank, cnt = _post(o, w_out_bf16, h, g1, sh2, sc2, w_router, b_router, seq)
    counts = cnt[0, :N_EXPERTS]
    padded = (counts + BM - 1) // BM * BM
    pend = jnp.cumsum(padded)
    pstart = pend - padded
    e_flat = idx[:, :TOP_K].reshape(-1)
    rank_flat = rank[:, :TOP_K].reshape(-1)
    n_blocks = (n_tok * TOP_K + N_EXPERTS * (BM - 1)) // BM
    block_start = jnp.arange(n_blocks, dtype=jnp.int32) * BM
    block_expert = jnp.minimum(
        jnp.sum((pend[None, :] <= block_start[:, None]).astype(jnp.int32), axis=1), N_EXPERTS - 1)
    n_valid = (pend[-1:] // BM).astype(jnp.int32)
    x_sorted = _dispatch(e_flat, rank_flat, pstart, pend, counts, xs, n_blocks * BM)
    y_sorted = _experts(block_expert, n_valid, counts, x_sorted, w_gate, b_gate, w_up, b_up, w_down,
                        b_down, layer)
    return _combine(e_flat, rank_flat, pstart, y_sorted, h1, gates, g2, seq)


def _group_rms(y, ind_ref, indt_ref):
    ss = jnp.dot((y * y).astype(BF16), ind_ref[...], preferred_element_type=F32)
    r = lax.rsqrt(ss * (1.0 / HALF) + EPS)
    r_hi, r_lo = _split(r)
    d = functools.partial(jnp.dot, preferred_element_type=F32)
    return y * (d(r_hi, indt_ref[...]) + d(r_lo, indt_ref[...]))


def _qkv_kernel(h_ref, ksh_ref, ksc_ref, qsh_ref, qsc_ref, wk_ref, wv_ref, wq_ref, kn_ref, qn_ref,
                ind_ref, indt_ref, q_ref, k_ref, v_ref):
    h = h_ref[...]
    hkv = _modulate(h, ksh_ref[0], ksc_ref[0]).astype(BF16)
    xn = _modulate(h, qsh_ref[0], qsc_ref[0]).astype(BF16)
    k = jnp.dot(hkv, wk_ref[...], preferred_element_type=F32)
    v = jnp.dot(hkv, wv_ref[...], preferred_element_type=F32)
    q = jnp.dot(xn, wq_ref[...], preferred_element_type=F32)
    k_ref[...] = (_group_rms(k, ind_ref, indt_ref) * kn_ref[...]).astype(BF16)
    q_ref[...] = (_group_rms(q, ind_ref, indt_ref) * qn_ref[...] * (HALF ** -0.5)).astype(BF16)
    v_ref[...] = v.astype(BF16)


def _qkv(h, kv_shift, kv_scale, q_shift, q_scale, w_k, w_v, w_q, k_norm, q_norm, seq):
    n_tok = h.shape[0]
    per_b = seq // TM
    vec = pl.BlockSpec((1, 1, D), lambda i: (i // per_b, 0, 0))
    row = pl.BlockSpec((TM, D), lambda i: (i, 0))
    full = lambda shape: pl.BlockSpec(shape, lambda i: (0, 0))
    groups = jnp.arange(D) // HALF
    ind = (groups[:, None] == jnp.arange(LANES)[None, :]).astype(BF16)
    out = jax.ShapeDtypeStruct((n_tok, D), BF16)
    return pl.pallas_call(
        _qkv_kernel,
        out_shape=(out, out, out),
        grid=(n_tok // TM,),
        in_specs=[row, vec, vec, vec, vec, full((D, D)), full((D, D)), full((D, D)),
                  full((1, D)), full((1, D)), full((D, LANES)), full((LANES, D))],
        out_specs=(row, row, row),
        compiler_params=_params(("arbitrary",)),
    )(h, kv_shift, kv_scale, q_shift, q_scale, w_k.astype(BF16), w_v.astype(BF16), w_q.astype(BF16),
      jnp.tile(k_norm, D // HALF).reshape(1, D), jnp.tile(q_norm, D // HALF).reshape(1, D),
      ind, ind.T)


def _attn_kernel(q_ref, k_ref, v_ref, lam_ref, sub_ref, o_ref, *, tq, lambda_init):
    qi = pl.program_id(2)
    half = tq // 2
    lane = lax.broadcasted_iota(jnp.int32, (half, HEAD), 1)
    ones = jnp.ones((tq, HEAD), BF16)
    shift = CHUNK.bit_length() - 1
    diag_ok = ((lax.broadcasted_iota(jnp.int32, (half, half), 1) >> shift)
               <= (lax.broadcasted_iota(jnp.int32, (half, half), 0) >> shift))

    units = [(slab, m) for slab in range(2) for m in range(2)]
    qs = {}
    for slab, m in units:
        q = q_ref[pl.ds(slab * half, half), :]
        qs[slab, m] = jnp.where((lane < HALF) if m == 0 else (lane >= HALF), q, jnp.zeros_like(q))

    def scores(u, k):
        return lax.dot_general(qs[u], k, (((1,), (1,)), ((), ())), preferred_element_type=F32)

    def update(s, v1, carry):
        m_new = [jnp.maximum(carry[i][0], jnp.max(s[i], axis=-1, keepdims=True)) for i in range(4)]
        alpha = [jnp.exp(carry[i][0] - m_new[i]) for i in range(4)]
        p = [jnp.exp(s[i] - m_new[i]).astype(BF16) for i in range(4)]
        acc = [alpha[i] * carry[i][1] + jnp.dot(p[i], v1[i], preferred_element_type=F32) for i in range(4)]
        return tuple((m_new[i], acc[i]) for i in range(4))

    def kv_tile(kj):
        start = pl.multiple_of(kj * tq, tq)
        k = k_ref[pl.ds(start, tq), :]
        v1 = jnp.concatenate([v_ref[pl.ds(start, tq), :], ones], axis=1)
        return k, v1

    def full_tile(kj, carry):
        k, v1 = kv_tile(kj)
        return update([scores(u, k) for u in units], [v1] * 4, carry)

    def diag_tile(carry):
        k, v1 = kv_tile(qi)
        s, vs = [], []
        for u in units:
            if u[0] == 0:
                s.append(jnp.where(diag_ok, scores(u, k[:half]), NEG))
                vs.append(v1[:half])
            else:
                full = scores(u, k)
                s.append(jnp.concatenate([full[:, :half], jnp.where(diag_ok, full[:, half:], NEG)], axis=1))
                vs.append(v1)
        return update(s, vs, carry)

    init = tuple((jnp.full((half, 1), NEG, F32), jnp.zeros((half, 2 * HEAD), F32)) for _ in units)
    carry = diag_tile(lax.fori_loop(0, qi, full_tile, init))

    lp = lam_ref[...]
    lam = (jnp.exp(jnp.sum(lp[0:1] * lp[1:2], axis=-1, keepdims=True))
           - jnp.exp(jnp.sum(lp[2:3] * lp[3:4], axis=-1, keepdims=True)) + lambda_init)
    for slab in range(2):
        acc0, acc1 = carry[2 * slab][1], carry[2 * slab + 1][1]
        o = acc0[:, :HEAD] / acc0[:, HEAD:HEAD + 1] - lam * (acc1[:, :HEAD] / acc1[:, HEAD:HEAD + 1])
        o = o * lax.rsqrt(jnp.mean(o * o, axis=-1, keepdims=True) + EPS) * sub_ref[...] * (1.0 - lambda_init)
        o_ref[pl.ds(slab * half, half), :] = o.astype(o_ref.dtype)


def _attention(q, k, v, lam_params, subln, bsz, seq, lambda_init):
    tq = TQ
    nq = seq // tq
    n_tok = bsz * seq
    k3 = k.reshape(bsz, seq, D)
    v3 = v.reshape(bsz, seq, D)
    kv_spec = pl.BlockSpec((None, seq, HEAD), lambda b, h, i: (b, 0, h))
    return pl.pallas_call(
        functools.partial(_attn_kernel, tq=tq, lambda_init=lambda_init),
        out_shape=jax.ShapeDtypeStruct((n_tok, D), BF16),
        grid=(bsz, N_HEADS, nq),
        in_specs=[pl.BlockSpec((tq, HEAD), lambda b, h, i: (b * nq + i, h)), kv_spec, kv_spec,
                  pl.BlockSpec((SUBLANES, LANES), lambda b, h, i: (0, 0)),
                  pl.BlockSpec((1, HEAD), lambda b, h, i: (0, 0))],
        out_specs=pl.BlockSpec((tq, HEAD), lambda b, h, i: (b * nq + i, h)),
        compiler_params=_params(("arbitrary", "arbitrary", "arbitrary")),
    )(q, k3, v3, lam_params, subln.reshape(1, HEAD))


def kernel(x, c, ada_w, ada_b, a_w_in, a_conv, a_log, a_dt_bias, a_norm, a_w_out, kv_ada_w, kv_ada_b,
           b_w_k, b_w_v, b_k_norm, b_w_q, b_q_norm, b_lam_q1, b_lam_k1, b_lam_q2, b_lam_k2, b_subln,
           b_w_out, moe_w_router, moe_b_router, moe_w_gate, moe_b_gate, moe_w_up, moe_b_up,
           moe_w_down, moe_b_down):
    bsz, seq, _ = x.shape
    h = x.reshape(bsz * seq, D)

    def six(mod):
        return [mod[:, i * D:(i + 1) * D].reshape(bsz, 1, D) for i in range(6)]

    moe_args = (moe_w_router, moe_b_router, moe_w_gate, moe_b_gate, moe_w_up, moe_b_up,
                moe_w_down, moe_b_down)

    def moe(o, w_out, h, g1, sh2, sc2, g2, layer):
        return _moe(o, w_out.astype(BF16), h, g1, sh2, sc2, g2, moe_w_router[layer],
                    moe_b_router[layer], *moe_args[2:], layer, seq)

    sh1, sc1, g1, sh2, sc2, g2 = six(_mod_vectors(c, ada_w, ada_b, 0))
    n_in = a_w_in.shape[-1]
    w_in = jnp.pad(a_w_in[0], ((0, 0), (0, IN_COLS - n_in))).astype(BF16)
    proj = _inproj(h, sh1, sc1, w_in, seq)
    dec = jnp.zeros((SUBLANES, LANES), F32)
    dec = dec.at[0, N_HEADS:2 * N_HEADS].set(a_log[0]).at[1, N_HEADS:2 * N_HEADS].set(a_dt_bias[0])
    o = _delta(proj, a_conv[0], dec, a_norm[0].reshape(1, HEAD), bsz, seq)
    h = moe(o, a_w_out[0], h, g1, sh2, sc2, g2, 0)

    kv_mod = _mod_vectors(c, kv_ada_w, kv_ada_b, None)
    kv_shift = kv_mod[:, :D].reshape(bsz, 1, D)
    kv_scale = kv_mod[:, D:].reshape(bsz, 1, D)
    sh1, sc1, g1, sh2, sc2, g2 = six(_mod_vectors(c, ada_w, ada_b, 1))
    q, k, v = _qkv(h, kv_shift, kv_scale, sh1, sc1, b_w_k, b_w_v, b_w_q[0], b_k_norm, b_q_norm[0], seq)

    lambda_init = 0.8 - 0.6 * math.exp(-0.3 * 1)
    lam_params = jnp.zeros((SUBLANES, LANES), F32)
    for r, p in enumerate((b_lam_q1, b_lam_k1, b_lam_q2, b_lam_k2)):
        lam_params = lam_params.at[r, :HALF].set(p[0])
    o = _attention(q, k, v, lam_params, b_subln[0], bsz, seq, lambda_init)
    h = moe(o, b_w_out[0], h, g1, sh2, sc2, g2, 1)
    return h.reshape(bsz, seq, D)
```

```python
import functools
import math

import jax
import jax.numpy as jnp
from jax import lax
from jax.experimental import pallas as pl
from jax.experimental.pallas import tpu as pltpu

F32 = jnp.float32
BF16 = jnp.bfloat16

D = 1024
CHUNK = 64
N_HEADS = 8
HEAD = 128
HALF = 64
CONV_W = 4
N_EXPERTS = 32
TOP_K = 4
SWIGLU_LIMIT = 7.0
SWIGLU_ALPHA = 1.702
EPS = 1e-6
NEG = -1e30

LANES = 128
SUBLANES = 8
SLABS = D // LANES
VMEM_LIMIT = 56 * 1024 * 1024

TM = 512
T_DELTA = 256
TQ = 512
BM = 512
TD = 128
ROW_UNROLL = 8
IN_COLS = 4224


def _dot(a, b):
    return jnp.dot(a.astype(BF16), b.astype(BF16), preferred_element_type=F32)


def _dot_nt(a, b):
    return lax.dot_general(a.astype(BF16), b.astype(BF16), (((1,), (1,)), ((), ())),
                           preferred_element_type=F32)


def _split(a):
    hi = a.astype(BF16)
    lo = (a - hi.astype(F32)).astype(BF16)
    return hi, lo


def _dot3(a, b):
    a_hi, a_lo = _split(a)
    b_hi, b_lo = _split(b)
    d = functools.partial(jnp.dot, preferred_element_type=F32)
    return d(a_hi, b_hi) + (d(a_hi, b_lo) + d(a_lo, b_hi))


def _sigmoid(x):
    return 1.0 / (1.0 + jnp.exp(-x))


def _silu(x):
    return x * _sigmoid(x)


def _modulate(h, shift, scale):
    ms = jnp.mean(h * h, axis=-1, keepdims=True)
    return h * lax.rsqrt(ms + EPS) * (1.0 + scale) + shift


def _params(sem):
    return pltpu.CompilerParams(dimension_semantics=sem, vmem_limit_bytes=VMEM_LIMIT)


def _mod_kernel(c_ref, w_ref, b_ref, o_ref):
    o_ref[...] = _dot3(_silu(c_ref[...]), w_ref[...]) + b_ref[...]


def _mod_vectors(c, w, b, layer):
    n = w.shape[-1]
    tn = 1024
    if layer is None:
        w_spec = pl.BlockSpec((D, tn), lambda j: (0, j))
        b2 = b.reshape(1, n)
        b_spec = pl.BlockSpec((1, tn), lambda j: (0, j))
    else:
        w_spec = pl.BlockSpec((None, D, tn), lambda j: (layer, 0, j))
        b2 = b.reshape(b.shape[0], 1, n)
        b_spec = pl.BlockSpec((None, 1, tn), lambda j: (layer, 0, j))
    bsz = c.shape[0]
    return pl.pallas_call(
        _mod_kernel,
        out_shape=jax.ShapeDtypeStruct((bsz, n), F32),
        grid=(n // tn,),
        in_specs=[pl.BlockSpec((bsz, D), lambda j: (0, 0)), w_spec, b_spec],
        out_specs=pl.BlockSpec((bsz, tn), lambda j: (0, j)),
        compiler_params=_params(("arbitrary",)),
    )(c, w, b2)


def _inproj_kernel(h_ref, sh_ref, sc_ref, w_ref, o_ref):
    xn = _modulate(h_ref[...], sh_ref[0], sc_ref[0])
    o_ref[...] = jnp.dot(xn.astype(BF16), w_ref[...], preferred_element_type=F32)


def _inproj(h, shift, scale, w_bf16, seq):
    n_tok = h.shape[0]
    per_b = seq // TM
    vec = pl.BlockSpec((1, 1, D), lambda i: (i // per_b, 0, 0))
    return pl.pallas_call(
        _inproj_kernel,
        out_shape=jax.ShapeDtypeStruct((n_tok, IN_COLS), F32),
        grid=(n_tok // TM,),
        in_specs=[pl.BlockSpec((TM, D), lambda i: (i, 0)), vec, vec,
                  pl.BlockSpec((D, IN_COLS), lambda i: (0, 0))],
        out_specs=pl.BlockSpec((TM, IN_COLS), lambda i: (i, 0)),
        compiler_params=_params(("arbitrary",)),
    )(h, shift, scale, w_bf16)


N_QKV = 3 * D


def _delta_kernel(qkv_ref, z_ref, sm_ref, cw_ref, dec_ref, nw_ref, tri_ref, o_ref, pad, state, *, t):
    @pl.when(pl.program_id(1) == 0)
    def _():
        pad[pl.ds(0, SUBLANES), :] = jnp.zeros((SUBLANES, N_QKV), F32)
        state[...] = jnp.zeros_like(state)

    pad[pl.ds(SUBLANES, t), :] = qkv_ref[...]

    sm = sm_ref[...]
    sig = _sigmoid(sm)
    dec = dec_ref[...]
    xs = sm + dec[1:2]
    softplus = jnp.maximum(xs, 0.0) + jnp.log(1.0 + jnp.exp(-jnp.abs(xs)))
    g_raw = -jnp.exp(dec[0:1]) * softplus
    g_cum = _dot3(tri_ref[...], g_raw)
    g_t = g_cum.T
    lane = lax.broadcasted_iota(jnp.int32, (t, LANES), 1)

    ri = lax.broadcasted_iota(jnp.int32, (CHUNK, CHUNK), 0)
    ci = lax.broadcasted_iota(jnp.int32, (CHUNK, CHUNK), 1)
    incl = ri >= ci
    strict = ri > ci
    eye = jnp.where(ri == ci, 1.0, 0.0)
    level_masks = []
    size = 2
    while size < CHUNK:
        shift = size.bit_length() - 1
        level_masks.append(((ri >> (shift + 1)) == (ci >> (shift + 1))) & ((ri >> shift) != (ci >> shift)))
        size *= 2
    nw = nw_ref[...]

    def conv_silu(col):
        cols = slice(col * HEAD, (col + 1) * HEAD)
        w = cw_ref[:, cols]
        y = w[3:4] * pad[pl.ds(SUBLANES, t), cols]
        for k in range(1, CONV_W):
            y = y + w[3 - k:4 - k] * pad[pl.ds(SUBLANES - k, t), cols]
        return _silu(y)

    n_chunks = t // CHUNK
    units = [(hd, c) for hd in range(N_HEADS) for c in range(n_chunks)]
    qs, ks, kbs, vbs, gcs, egs, grs = {}, {}, {}, {}, {}, {}, {}
    for hd in range(N_HEADS):
        q = conv_silu(hd)
        k = conv_silu(N_HEADS + hd)
        v = conv_silu(2 * N_HEADS + hd)
        q = q * lax.rsqrt(jnp.sum(q * q, axis=-1, keepdims=True) + EPS) * (HEAD ** -0.5)
        k = k * lax.rsqrt(jnp.sum(k * k, axis=-1, keepdims=True) + EPS)
        beta = jnp.sum(jnp.where(lane == hd, sig, 0.0), axis=-1, keepdims=True)
        g_col = jnp.sum(jnp.where(lane == hd + N_HEADS, g_cum, 0.0), axis=-1, keepdims=True)
        g_row = g_t[N_HEADS + hd:N_HEADS + hd + 1, :]
        kb = k * beta
        vb = v * beta
        eg = jnp.exp(g_col)
        for c in range(n_chunks):
            rows = slice(c * CHUNK, (c + 1) * CHUNK)
            u = (hd, c)
            qs[u], ks[u], kbs[u], vbs[u], gcs[u], egs[u] = q[rows], k[rows], kb[rows], vb[rows], g_col[rows], eg[rows]
            grs[u] = g_row[:, c * CHUNK:(c + 1) * CHUNK]

    decay = {u: jnp.where(incl, jnp.exp(jnp.where(incl, gcs[u] - grs[u], 0.0)), 0.0) for u in units}
    both = {u: _dot_nt(jnp.concatenate([kbs[u], qs[u]], axis=0), ks[u]) for u in units}
    l_mat = {u: jnp.where(strict, both[u][:CHUNK] * decay[u], 0.0) for u in units}
    qk = {u: both[u][CHUNK:] * decay[u] for u in units}
    t_inv = {u: eye - jnp.where((ri ^ ci) == 1, l_mat[u], 0.0) for u in units}
    for pair in level_masks:
        xc = {u: _dot(t_inv[u], jnp.where(pair, l_mat[u], 0.0)) for u in units}
        t_inv = {u: t_inv[u] - _dot(xc[u], t_inv[u]) for u in units}
    uw = {u: _dot(t_inv[u], jnp.concatenate([vbs[u], kbs[u] * egs[u]], axis=1)) for u in units}
    g_last = {u: gcs[u][CHUNK - 1:CHUNK] for u in units}
    k_dec_t = {u: (ks[u] * jnp.exp(g_last[u] - gcs[u])).T for u in units}
    q_dec = {u: qs[u] * egs[u] for u in units}

    heads = range(N_HEADS)
    s = [state[hd] for hd in heads]
    for c in range(n_chunks):
        ws = [_dot(jnp.concatenate([uw[hd, c][:, HEAD:], q_dec[hd, c]], axis=0), s[hd]) for hd in heads]
        v_new = [uw[hd, c][:, :HEAD] - ws[hd][:CHUNK] for hd in heads]
        o = [ws[hd][CHUNK:] + _dot(qk[hd, c], v_new[hd]) for hd in heads]
        s = [s[hd] * jnp.exp(g_last[hd, c]) + _dot(k_dec_t[hd, c], v_new[hd]) for hd in heads]
        for hd in heads:
            zc = z_ref[pl.ds(c * CHUNK, CHUNK), hd * HEAD:(hd + 1) * HEAD]
            on = o[hd] * lax.rsqrt(jnp.mean(o[hd] * o[hd], axis=-1, keepdims=True) + EPS) * nw * _silu(zc)
            o_ref[pl.ds(c * CHUNK, CHUNK), hd * HEAD:(hd + 1) * HEAD] = on.astype(o_ref.dtype)
    for hd in heads:
        state[hd] = s[hd]

    pad[pl.ds(0, SUBLANES), :] = pad[pl.ds(t, SUBLANES), :]


def _chunk_tri(t):
    r = jnp.arange(t)
    same = (r[:, None] // CHUNK) == (r[None, :] // CHUNK)
    return jnp.where(same & (r[:, None] >= r[None, :]), 1.0, 0.0).astype(F32)


def _delta(proj, conv_w, dec, norm_w, bsz, seq):
    t = T_DELTA
    nt = seq // t
    n_tok = bsz * seq
    const = lambda shape: pl.BlockSpec(shape, lambda b, j: (0, 0))
    return pl.pallas_call(
        functools.partial(_delta_kernel, t=t),
        out_shape=jax.ShapeDtypeStruct((n_tok, D), BF16),
        grid=(bsz, nt),
        in_specs=[pl.BlockSpec((t, N_QKV), lambda b, j: (b * nt + j, 0)),
                  pl.BlockSpec((t, D), lambda b, j: (b * nt + j, N_QKV // D)),
                  pl.BlockSpec((t, LANES), lambda b, j: (b * nt + j, (N_QKV + D) // LANES)),
                  const((CONV_W, N_QKV)), const((SUBLANES, LANES)), const((1, HEAD)), const((t, t))],
        out_specs=pl.BlockSpec((t, D), lambda b, j: (b * nt + j, 0)),
        scratch_shapes=[pltpu.VMEM((t + SUBLANES, N_QKV), F32),
                        pltpu.VMEM((N_HEADS, HEAD, HEAD), F32)],
        compiler_params=_params(("arbitrary", "arbitrary")),
    )(proj, proj, proj, conv_w, dec, norm_w, _chunk_tri(t))


def _post_kernel(o_ref, w_ref, h_ref, g_ref, sh_ref, sc_ref, wr_ref, br_ref, tri_ref,
                 h1_ref, xs_ref, idx_ref, gate_ref, rank_ref, cnt_ref, carry, *, tm):
    @pl.when(pl.program_id(0) == 0)
    def _():
        carry[...] = jnp.zeros_like(carry)

    y = jnp.dot(o_ref[...], w_ref[...], preferred_element_type=F32)
    h1 = h_ref[...] + g_ref[0] * y
    h1_ref[...] = h1
    xn = _modulate(h1, sh_ref[0], sc_ref[0])
    for s in range(SLABS):
        xs_ref[pl.ds(s, tm, stride=SLABS), :] = xn[:, s * LANES:(s + 1) * LANES]

    lane = lax.broadcasted_iota(jnp.int32, (tm, LANES), 1)
    lane_f = lane.astype(F32)
    logits = jnp.where(lane < N_EXPERTS, _dot3(xn, wr_ref[...]) + br_ref[...], NEG)
    vals, onehots = [], []
    for _ in range(TOP_K):
        m = jnp.max(logits, axis=-1, keepdims=True)
        first = jnp.min(jnp.where(logits == m, lane_f, float(LANES)), axis=-1, keepdims=True)
        hit = lane_f == first
        vals.append(m)
        onehots.append(hit)
        logits = jnp.where(hit, 2.0 * NEG, logits)
    exps = [jnp.exp(v - vals[0]) for v in vals]
    denom = exps[0] + exps[1] + exps[2] + exps[3]

    member = jnp.zeros((tm, LANES), F32)
    for hit in onehots:
        member = jnp.where(hit, 1.0, member)
    before = jnp.dot(tri_ref[...], member.astype(BF16), preferred_element_type=F32) + carry[...]
    idx_out = jnp.zeros((tm, LANES), F32)
    gate_out = jnp.zeros((tm, LANES), F32)
    rank_out = jnp.zeros((tm, LANES), F32)
    for k in range(TOP_K):
        first = jnp.sum(jnp.where(onehots[k], lane_f, 0.0), axis=-1, keepdims=True)
        rank = jnp.sum(jnp.where(onehots[k], before, 0.0), axis=-1, keepdims=True)
        idx_out = jnp.where(lane == k, first, idx_out)
        gate_out = jnp.where(lane == k, exps[k] / denom, gate_out)
        rank_out = jnp.where(lane == k, rank, rank_out)
    idx_ref[...] = idx_out.astype(jnp.int32)
    gate_ref[...] = gate_out
    rank_ref[...] = rank_out.astype(jnp.int32)
    carry[...] = carry[...] + jnp.sum(member, axis=0, keepdims=True)
    cnt_ref[...] = jnp.broadcast_to(carry[...], cnt_ref.shape).astype(jnp.int32)


def _post(o, w_bf16, h, gate, shift, scale, w_router, b_router, seq):
    n_tok = h.shape[0]
    tm = TM
    per_b = seq // tm
    vec = pl.BlockSpec((1, 1, D), lambda i: (i // per_b, 0, 0))
    row = pl.BlockSpec((tm, D), lambda i: (i, 0))
    small = pl.BlockSpec((tm, LANES), lambda i: (i, 0))
    r = jnp.arange(tm)
    tri = jnp.where(r[:, None] > r[None, :], 1.0, 0.0).astype(BF16)
    wr = jnp.pad(w_router, ((0, 0), (0, LANES - N_EXPERTS)))
    br = jnp.pad(b_router, (0, LANES - N_EXPERTS)).reshape(1, LANES)
    return pl.pallas_call(
        functools.partial(_post_kernel, tm=tm),
        out_shape=(jax.ShapeDtypeStruct((n_tok, D), F32),
                   jax.ShapeDtypeStruct((n_tok * SLABS, LANES), F32),
                   jax.ShapeDtypeStruct((n_tok, LANES), jnp.int32),
                   jax.ShapeDtypeStruct((n_tok, LANES), F32),
                   jax.ShapeDtypeStruct((n_tok, LANES), jnp.int32),
                   jax.ShapeDtypeStruct((SUBLANES, LANES), jnp.int32)),
        grid=(n_tok // tm,),
        in_specs=[row, pl.BlockSpec((D, D), lambda i: (0, 0)), row, vec, vec, vec,
                  pl.BlockSpec((D, LANES), lambda i: (0, 0)),
                  pl.BlockSpec((1, LANES), lambda i: (0, 0)),
                  pl.BlockSpec((tm, tm), lambda i: (0, 0))],
        out_specs=(row, pl.BlockSpec((tm * SLABS, LANES), lambda i: (i, 0)), small, small, small,
                   pl.BlockSpec((SUBLANES, LANES), lambda i: (0, 0))),
        scratch_shapes=[pltpu.VMEM((1, LANES), F32)],
        compiler_params=_params(("arbitrary",)),
    )(o, w_bf16, h, gate, shift, scale, wr, br, tri)


def _row_copy(src, src_row, dst, dst_row, sem):
    return pltpu.make_async_copy(src.at[pl.ds(pl.multiple_of(src_row * SLABS, SLABS), SLABS)],
                                 dst.at[pl.ds(pl.multiple_of(dst_row * SLABS, SLABS), SLABS)], sem)


def _dispatch_kernel(pos_ref, pend_ref, cnt_ref, x_ref, out_ref, buf, zbuf, sems, zsem, *, td, n_tiles,
                     n_blocks):
    i = pl.program_id(0)
    slot = i % 2

    def wait_slot(s):
        for _ in range(TOP_K):
            pltpu.make_async_copy(buf.at[s], out_ref.at[pl.ds(0, td * SLABS)], sems.at[s]).wait()

    @pl.when(i == 0)
    def _():
        zbuf[...] = jnp.zeros_like(zbuf)

        def zero_copy(e):
            last = pl.multiple_of((pend_ref[e] - BM) * SLABS, BM * SLABS)
            return pltpu.make_async_copy(zbuf, out_ref.at[pl.ds(last, BM * SLABS)], zsem)

        def tail_copy(j):
            start = pl.multiple_of(j * (BM * SLABS), BM * SLABS)
            return pltpu.make_async_copy(zbuf, out_ref.at[pl.ds(start, BM * SLABS)], zsem)

        def tail_start(j, c):
            tail_copy(j).start()
            return c

        def tail_wait(j, c):
            tail_copy(j).wait()
            return c

        n_valid = pend_ref[N_EXPERTS - 1] // BM
        for e in range(N_EXPERTS):
            @pl.when(cnt_ref[e] > 0)
            def _():
                zero_copy(e).start()
        lax.fori_loop(n_valid, n_blocks, tail_start, 0)
        for e in range(N_EXPERTS):
            @pl.when(cnt_ref[e] > 0)
            def _():
                zero_copy(e).wait()
        lax.fori_loop(n_valid, n_blocks, tail_wait, 0)

    @pl.when(i >= 2)
    def _():
        wait_slot(slot)

    buf[slot] = x_ref[...]

    def body(g, carry):
        for u in range(ROW_UNROLL):
            t = g * ROW_UNROLL + u
            for k in range(TOP_K):
                p = pos_ref[(i * td + t) * TOP_K + k]
                _row_copy(buf.at[slot], t, out_ref, p, sems.at[slot]).start()
        return carry

    lax.fori_loop(0, td // ROW_UNROLL, body, 0)

    @pl.when(i == n_tiles - 1)
    def _():
        wait_slot(slot)
        if n_tiles > 1:
            wait_slot(1 - slot)


def _dispatch(pos_flat, pend, counts, xs, n_rows):
    n_tok = xs.shape[0] // SLABS
    td = TD
    n_tiles = n_tok // td
    return pl.pallas_call(
        functools.partial(_dispatch_kernel, td=td, n_tiles=n_tiles, n_blocks=n_rows // BM),
        out_shape=jax.ShapeDtypeStruct((n_rows * SLABS, LANES), F32),
        grid_spec=pltpu.PrefetchScalarGridSpec(
            num_scalar_prefetch=3,
            grid=(n_tiles,),
            in_specs=[pl.BlockSpec((td * SLABS, LANES), lambda i, *_: (i, 0))],
            out_specs=pl.BlockSpec(memory_space=pl.ANY),
            scratch_shapes=[pltpu.VMEM((2, td * SLABS, LANES), F32),
                            pltpu.VMEM((BM * SLABS, LANES), F32),
                            pltpu.SemaphoreType.DMA((2,)),
                            pltpu.SemaphoreType.DMA(())]),
        compiler_params=_params(("arbitrary",)),
    )(pos_flat, pend, counts, xs)


def _expert_kernel(be_ref, nv_ref, first_ref, par_ref, nxt_ref, x_ref, wg_ref, bg_ref, wu_ref, bu_ref,
                   wd_ref, bd_ref, y_ref, wst, wg_s, wu_s, wd_s, wsem, *, layer):
    j = pl.program_id(0)

    def weight_copies(e, s):
        return [pltpu.make_async_copy(w.at[layer, e], wst.at[s, i], wsem.at[s])
                for i, w in enumerate((wg_ref, wu_ref, wd_ref))]

    @pl.when(j >= nv_ref[0])
    def _():
        y_ref[...] = jnp.zeros_like(y_ref)

    @pl.when(j < nv_ref[0])
    def _():
        @pl.when(first_ref[j] == 1)
        def _():
            s = par_ref[j]

            @pl.when(j == 0)
            def _():
                for c in weight_copies(be_ref[0], s):
                    c.start()

            for c in weight_copies(be_ref[j], s):
                c.wait()
            wg_s[...] = wst[s, 0].astype(BF16)
            wu_s[...] = wst[s, 1].astype(BF16)
            wd_s[...] = wst[s, 2].astype(BF16)

            @pl.when(nxt_ref[j] >= 0)
            def _():
                for c in weight_copies(nxt_ref[j], 1 - s):
                    c.start()

        x = jnp.concatenate([x_ref[pl.ds(s, BM, stride=SLABS), :] for s in range(SLABS)],
                            axis=1).astype(BF16)
        hg = jnp.dot(x, wg_s[...], preferred_element_type=F32) + bg_ref[...]
        hu = jnp.dot(x, wu_s[...], preferred_element_type=F32) + bu_ref[...]
        g = jnp.minimum(hg, SWIGLU_LIMIT)
        u = jnp.clip(hu, -SWIGLU_LIMIT, SWIGLU_LIMIT)
        act = g * _sigmoid(SWIGLU_ALPHA * g) * (u + 1.0)
        y = jnp.dot(act.astype(BF16), wd_s[...], preferred_element_type=F32) + bd_ref[...]
        for s in range(SLABS):
            y_ref[pl.ds(s, BM, stride=SLABS), :] = y[:, s * LANES:(s + 1) * LANES]


def _experts(block_expert, n_valid, counts, x_sorted, w_gate, b_gate, w_up, b_up, w_down, b_down, layer):
    n_blocks = block_expert.shape[0]
    blocks = jnp.arange(n_blocks, dtype=jnp.int32)
    valid = blocks < n_valid[0]
    first = valid & ((blocks == 0) | (block_expert != jnp.roll(block_expert, 1)))
    parity = (jnp.cumsum(first.astype(jnp.int32)) - 1) % 2
    experts = jnp.arange(N_EXPERTS, dtype=jnp.int32)
    later_used = (counts[None, :] > 0) & (experts[None, :] > experts[:, None])
    next_used = jnp.min(jnp.where(later_used, experts[None, :], N_EXPERTS), axis=1)
    next_expert = jnp.where(next_used < N_EXPERTS, next_used, -1)[block_expert]

    def blk(j, be, nv, *_):
        return (jnp.minimum(j, nv[0] - 1), 0)

    hbm = pl.BlockSpec(memory_space=pl.ANY)
    b_spec = pl.BlockSpec((None, None, 1, D), lambda j, be, *_: (layer, be[j], 0, 0))
    r4 = lambda b: b.reshape(b.shape[0], N_EXPERTS, 1, D)
    return pl.pallas_call(
        functools.partial(_expert_kernel, layer=layer),
        out_shape=jax.ShapeDtypeStruct(x_sorted.shape, F32),
        grid_spec=pltpu.PrefetchScalarGridSpec(
            num_scalar_prefetch=5,
            grid=(n_blocks,),
            in_specs=[pl.BlockSpec((BM * SLABS, LANES), blk), hbm, b_spec, hbm, b_spec, hbm, b_spec],
            out_specs=pl.BlockSpec((BM * SLABS, LANES), lambda j, *_: (j, 0)),
            scratch_shapes=[pltpu.VMEM((2, 3, D, D), F32),
                            pltpu.VMEM((D, D), BF16), pltpu.VMEM((D, D), BF16), pltpu.VMEM((D, D), BF16),
                            pltpu.SemaphoreType.DMA((2,))]),
        compiler_params=_params(("arbitrary",)),
    )(block_expert, n_valid, first.astype(jnp.int32), parity.astype(jnp.int32), next_expert.astype(jnp.int32),
      x_sorted, w_gate, r4(b_gate), w_up, r4(b_up), w_down, r4(b_down))


def _combine_kernel(pos_ref, y_ref, h_ref, gate_ref, g2_ref, o_ref, buf, sems, *, td, n_tiles):
    i = pl.program_id(0)
    slot = i % 2

    def issue(tile, s):
        def body(g, carry):
            for u in range(ROW_UNROLL):
                t = g * ROW_UNROLL + u
                for k in range(TOP_K):
                    p = pos_ref[(tile * td + t) * TOP_K + k]
                    _row_copy(y_ref, p, buf.at[s, k], t, sems.at[s]).start()
            return carry

        lax.fori_loop(0, td // ROW_UNROLL, body, 0)

    @pl.when(i == 0)
    def _():
        issue(0, 0)

    @pl.when(i + 1 < n_tiles)
    def _():
        issue(i + 1, 1 - slot)

    for k in range(TOP_K):
        pltpu.make_async_copy(y_ref.at[pl.ds(0, td * SLABS)], buf.at[slot, k], sems.at[slot]).wait()

    gates = gate_ref[...]
    acc = jnp.zeros((td, D), F32)
    for k in range(TOP_K):
        yk = jnp.concatenate([buf[slot, k, pl.ds(s, td, stride=SLABS), :] for s in range(SLABS)],
                             axis=1)
        acc = acc + gates[:, k:k + 1] * yk
    o_ref[...] = h_ref[...] + g2_ref[0] * acc


def _combine(pos_flat, y_sorted, h1, gates, g2, seq):
    n_tok = h1.shape[0]
    td = TD
    n_tiles = n_tok // td
    per_b = seq // td
    return pl.pallas_call(
        functools.partial(_combine_kernel, td=td, n_tiles=n_tiles),
        out_shape=jax.ShapeDtypeStruct((n_tok, D), F32),
        grid_spec=pltpu.PrefetchScalarGridSpec(
            num_scalar_prefetch=1,
            grid=(n_tiles,),
            in_specs=[pl.BlockSpec(memory_space=pl.ANY),
                      pl.BlockSpec((td, D), lambda i, p: (i, 0)),
                      pl.BlockSpec((td, LANES), lambda i, p: (i, 0)),
                      pl.BlockSpec((1, 1, D), lambda i, p: (i // per_b, 0, 0))],
            out_specs=pl.BlockSpec((td, D), lambda i, p: (i, 0)),
            scratch_shapes=[pltpu.VMEM((2, TOP_K, td * SLABS, LANES), F32),
                            pltpu.SemaphoreType.DMA((2,))]),
        compiler_params=_params(("arbitrary",)),
    )(pos_flat, y_sorted, h1, gates, g2)


def _moe(o, w_out_bf16, h, g1, sh2, sc2, g2, w_router, b_router, w_gate, b_gate, w_up, b_up,
         w_down, b_down, layer, seq):
    n_tok = h.shape[0]
    h1, xs, idx, gates, rank, cnt = _post(o, w_out_bf16, h, g1, sh2, sc2, w_router, b_router, seq)
    counts = cnt[0, :N_EXPERTS]
    padded = (counts + BM - 1) // BM * BM
    pend = jnp.cumsum(padded)
    pstart = pend - padded
    e_idx = idx[:, :TOP_K]
    pos_flat = (pstart[e_idx] + rank[:, :TOP_K]).reshape(-1)
    n_blocks = (n_tok * TOP_K + N_EXPERTS * (BM - 1)) // BM
    block_start = jnp.arange(n_blocks, dtype=jnp.int32) * BM
    block_expert = jnp.minimum(
        jnp.sum((pend[None, :] <= block_start[:, None]).astype(jnp.int32), axis=1), N_EXPERTS - 1)
    n_valid = (pend[-1:] // BM).astype(jnp.int32)
    x_sorted = _dispatch(pos_flat, pend, counts, xs, n_blocks * BM)
    y_sorted = _experts(block_expert, n_valid, counts, x_sorted, w_gate, b_gate, w_up, b_up, w_down,
                        b_down, layer)
    return _combine(pos_flat, y_sorted, h1, gates, g2, seq)


def _group_rms(y, ind_ref, indt_ref):
    ss = jnp.dot((y * y).astype(BF16), ind_ref[...], preferred_element_type=F32)
    r = lax.rsqrt(ss * (1.0 / HALF) + EPS)
    r_hi, r_lo = _split(r)
    d = functools.partial(jnp.dot, preferred_element_type=F32)
    return y * (d(r_hi, indt_ref[...]) + d(r_lo, indt_ref[...]))


def _qkv_kernel(h_ref, ksh_ref, ksc_ref, qsh_ref, qsc_ref, wk_ref, wv_ref, wq_ref, kn_ref, qn_ref,
                ind_ref, indt_ref, q_ref, k_ref, v_ref):
    h = h_ref[...]
    hkv = _modulate(h, ksh_ref[0], ksc_ref[0]).astype(BF16)
    xn = _modulate(h, qsh_ref[0], qsc_ref[0]).astype(BF16)
    k = jnp.dot(hkv, wk_ref[...], preferred_element_type=F32)
    v = jnp.dot(hkv, wv_ref[...], preferred_element_type=F32)
    q = jnp.dot(xn, wq_ref[...], preferred_element_type=F32)
    k_ref[...] = (_group_rms(k, ind_ref, indt_ref) * kn_ref[...]).astype(BF16)
    q_ref[...] = (_group_rms(q, ind_ref, indt_ref) * qn_ref[...] * (HALF ** -0.5)).astype(BF16)
    v_ref[...] = v.astype(BF16)


def _qkv(h, kv_shift, kv_scale, q_shift, q_scale, w_k, w_v, w_q, k_norm, q_norm, seq):
    n_tok = h.shape[0]
    per_b = seq // TM
    vec = pl.BlockSpec((1, 1, D), lambda i: (i // per_b, 0, 0))
    row = pl.BlockSpec((TM, D), lambda i: (i, 0))
    full = lambda shape: pl.BlockSpec(shape, lambda i: (0, 0))
    groups = jnp.arange(D) // HALF
    ind = (groups[:, None] == jnp.arange(LANES)[None, :]).astype(BF16)
    out = jax.ShapeDtypeStruct((n_tok, D), BF16)
    return pl.pallas_call(
        _qkv_kernel,
        out_shape=(out, out, out),
        grid=(n_tok // TM,),
        in_specs=[row, vec, vec, vec, vec, full((D, D)), full((D, D)), full((D, D)),
                  full((1, D)), full((1, D)), full((D, LANES)), full((LANES, D))],
        out_specs=(row, row, row),
        compiler_params=_params(("arbitrary",)),
    )(h, kv_shift, kv_scale, q_shift, q_scale, w_k.astype(BF16), w_v.astype(BF16), w_q.astype(BF16),
      jnp.tile(k_norm, D // HALF).reshape(1, D), jnp.tile(q_norm, D // HALF).reshape(1, D),
      ind, ind.T)


def _attn_kernel(q_ref, k_ref, v_ref, lam_ref, sub_ref, o_ref, *, tq, lambda_init):
    qi = pl.program_id(2)
    half = tq // 2
    lane = lax.broadcasted_iota(jnp.int32, (half, HEAD), 1)
    ones = jnp.ones((tq, HEAD), BF16)
    shift = CHUNK.bit_length() - 1
    diag_ok = ((lax.broadcasted_iota(jnp.int32, (half, half), 1) >> shift)
               <= (lax.broadcasted_iota(jnp.int32, (half, half), 0) >> shift))

    units = [(slab, m) for slab in range(2) for m in range(2)]
    qs = {}
    for slab, m in units:
        q = q_ref[pl.ds(slab * half, half), :]
        qs[slab, m] = jnp.where((lane < HALF) if m == 0 else (lane >= HALF), q, jnp.zeros_like(q))

    def scores(u, k):
        return lax.dot_general(qs[u], k, (((1,), (1,)), ((), ())), preferred_element_type=F32)

    def update(s, v1, carry):
        m_new = [jnp.maximum(carry[i][0], jnp.max(s[i], axis=-1, keepdims=True)) for i in range(4)]
        alpha = [jnp.exp(carry[i][0] - m_new[i]) for i in range(4)]
        p = [jnp.exp(s[i] - m_new[i]).astype(BF16) for i in range(4)]
        acc = [alpha[i] * carry[i][1] + jnp.dot(p[i], v1[i], preferred_element_type=F32) for i in range(4)]
        return tuple((m_new[i], acc[i]) for i in range(4))

    def kv_tile(kj):
        start = pl.multiple_of(kj * tq, tq)
        k = k_ref[pl.ds(start, tq), :]
        v1 = jnp.concatenate([v_ref[pl.ds(start, tq), :], ones], axis=1)
        return k, v1

    def full_tile(kj, carry):
        k, v1 = kv_tile(kj)
        return update([scores(u, k) for u in units], [v1] * 4, carry)

    def diag_tile(carry):
        k, v1 = kv_tile(qi)
        s, vs = [], []
        for u in units:
            if u[0] == 0:
                s.append(jnp.where(diag_ok, scores(u, k[:half]), NEG))
                vs.append(v1[:half])
            else:
                full = scores(u, k)
                s.append(jnp.concatenate([full[:, :half], jnp.where(diag_ok, full[:, half:], NEG)], axis=1))
                vs.append(v1)
        return update(s, vs, carry)

    init = tuple((jnp.full((half, 1), NEG, F32), jnp.zeros((half, 2 * HEAD), F32)) for _ in units)
    carry = diag_tile(lax.fori_loop(0, qi, full_tile, init))

    lp = lam_ref[...]
    lam = (jnp.exp(jnp.sum(lp[0:1] * lp[1:2], axis=-1, keepdims=True))
           - jnp.exp(jnp.sum(lp[2:3] * lp[3:4], axis=-1, keepdims=True)) + lambda_init)
    for slab in range(2):
        acc0, acc1 = carry[2 * slab][1], carry[2 * slab + 1][1]
        o = acc0[:, :HEAD] / acc0[:, HEAD:HEAD + 1] - lam * (acc1[:, :HEAD] / acc1[:, HEAD:HEAD + 1])
        o = o * lax.rsqrt(jnp.mean(o * o, axis=-1, keepdims=True) + EPS) * sub_ref[...] * (1.0 - lambda_init)
        o_ref[pl.ds(slab * half, half), :] = o.astype(o_ref.dtype)


def _attention(q, k, v, lam_params, subln, bsz, seq, lambda_init):
    tq = TQ
    nq = seq // tq
    n_tok = bsz * seq
    k3 = k.reshape(bsz, seq, D)
    v3 = v.reshape(bsz, seq, D)
    kv_spec = pl.BlockSpec((None, seq, HEAD), lambda b, h, i: (b, 0, h))
    return pl.pallas_call(
        functools.partial(_attn_kernel, tq=tq, lambda_init=lambda_init),
        out_shape=jax.ShapeDtypeStruct((n_tok, D), BF16),
        grid=(bsz, N_HEADS, nq),
        in_specs=[pl.BlockSpec((tq, HEAD), lambda b, h, i: (b * nq + i, h)), kv_spec, kv_spec,
                  pl.BlockSpec((SUBLANES, LANES), lambda b, h, i: (0, 0)),
                  pl.BlockSpec((1, HEAD), lambda b, h, i: (0, 0))],
        out_specs=pl.BlockSpec((tq, HEAD), lambda b, h, i: (b * nq + i, h)),
        compiler_params=_params(("arbitrary", "arbitrary", "arbitrary")),
    )(q, k3, v3, lam_params, subln.reshape(1, HEAD))


def kernel(x, c, ada_w, ada_b, a_w_in, a_conv, a_log, a_dt_bias, a_norm, a_w_out, kv_ada_w, kv_ada_b,
           b_w_k, b_w_v, b_k_norm, b_w_q, b_q_norm, b_lam_q1, b_lam_k1, b_lam_q2, b_lam_k2, b_subln,
           b_w_out, moe_w_router, moe_b_router, moe_w_gate, moe_b_gate, moe_w_up, moe_b_up,
           moe_w_down, moe_b_down):
    bsz, seq, _ = x.shape
    h = x.reshape(bsz * seq, D)

    def six(mod):
        return [mod[:, i * D:(i + 1) * D].reshape(bsz, 1, D) for i in range(6)]

    moe_args = (moe_w_router, moe_b_router, moe_w_gate, moe_b_gate, moe_w_up, moe_b_up,
                moe_w_down, moe_b_down)

    def moe(o, w_out, h, g1, sh2, sc2, g2, layer):
        return _moe(o, w_out.astype(BF16), h, g1, sh2, sc2, g2, moe_w_router[layer],
                    moe_b_router[layer], *moe_args[2:], layer, seq)

    sh1, sc1, g1, sh2, sc2, g2 = six(_mod_vectors(c, ada_w, ada_b, 0))
    n_in = a_w_in.shape[-1]
    w_in = jnp.pad(a_w_in[0], ((0, 0), (0, IN_COLS - n_in))).astype(BF16)
    proj = _inproj(h, sh1, sc1, w_in, seq)
    dec = jnp.zeros((SUBLANES, LANES), F32)
    dec = dec.at[0, N_HEADS:2 * N_HEADS].set(a_log[0]).at[1, N_HEADS:2 * N_HEADS].set(a_dt_bias[0])
    o = _delta(proj, a_conv[0], dec, a_norm[0].reshape(1, HEAD), bsz, seq)
    h = moe(o, a_w_out[0], h, g1, sh2, sc2, g2, 0)

    kv_mod = _mod_vectors(c, kv_ada_w, kv_ada_b, None)
    kv_shift = kv_mod[:, :D].reshape(bsz, 1, D)
    kv_scale = kv_mod[:, D:].reshape(bsz, 1, D)
    sh1, sc1, g1, sh2, sc2, g2 = six(_mod_vectors(c, ada_w, ada_b, 1))
    q, k, v = _qkv(h, kv_shift, kv_scale, sh1, sc1, b_w_k, b_w_v, b_w_q[0], b_k_norm, b_q_norm[0], seq)

    lambda_init = 0.8 - 0.6 * math.exp(-0.3 * 1)
    lam_params = jnp.zeros((SUBLANES, LANES), F32)
    for r, p in enumerate((b_lam_q1, b_lam_k1, b_lam_q2, b_lam_k2)):
        lam_params = lam_params.at[r, :HALF].set(p[0])
    o = _attention(q, k, v, lam_params, b_subln[0], bsz, seq, lambda_init)
    h = moe(o, b_w_out[0], h, g1, sh2, sc2, g2, 1)
    return h.reshape(bsz, seq, D)
```
